```python
import math
import jax, jax.numpy as jnp
from jax import lax
import numpy as np

D_MODEL = 2048
BATCH = 4
SEQ = 8192
DEPTH = 2

PLE_DIM = 256
D_FF = 5632
MIX_WIDTH = D_MODEL
GROUP_WIDTH = MIX_WIDTH // 4

A_HEADS = 4
A_HEAD_DIM = GROUP_WIDTH // (2 * A_HEADS)
A_QBLOCK = 128
N_BUCKETS = 32
MAX_DISTANCE = 128

B_HEADS = 4
B_HEAD_DIM = GROUP_WIDTH // B_HEADS
B_CHUNK = 128
ROPE_BASE = 10000.0

C_GROUPS = 4
C_CHUNK = 128
C_GROUP_DIM = GROUP_WIDTH // C_GROUPS

D_HEADS = 4
D_EXPAND = 128
D_HEAD_DIM = GROUP_WIDTH // D_HEADS
D_CHUNK = 64

A_COLS = 3 * GROUP_WIDTH
B_COLS = 4 * GROUP_WIDTH
C_COLS = 2 * GROUP_WIDTH
D_COLS = 4 * GROUP_WIDTH
IN_COLS = A_COLS + B_COLS + C_COLS + D_COLS

ALPHA = (2 * DEPTH) ** 0.25
BETA = (8 * DEPTH) ** -0.25
LN_EPS = 1e-5
MASK_VALUE = -1e30
LB_FLOOR = 1e-30

kernel_name = "hybrid_parallel_heads_diffattn_retnet_gmlp_hgrn2"


def layer_norm(x, g, b):
    xf = x.astype(jnp.float32)
    mu = jnp.mean(xf, axis=-1, keepdims=True)
    var = jnp.mean(jnp.square(xf - mu), axis=-1, keepdims=True)
    return ((xf - mu) * lax.rsqrt(var + LN_EPS) * g.astype(jnp.float32) + b.astype(jnp.float32)).astype(x.dtype)


def head_norm(x):
    xf = x.astype(jnp.float32)
    mu = jnp.mean(xf, axis=-1, keepdims=True)
    var = jnp.mean(jnp.square(xf - mu), axis=-1, keepdims=True)
    return ((xf - mu) * lax.rsqrt(var + LN_EPS)).astype(x.dtype)


def rms_norm(x, g):
    xf = x.astype(jnp.float32)
    return (xf * lax.rsqrt(jnp.mean(xf * xf, axis=-1, keepdims=True) + LN_EPS) * g.astype(jnp.float32)).astype(x.dtype)


def swiglu(x, w_in, w_out):
    gate, up = jnp.split(x @ w_in, 2, axis=-1)
    return (jax.nn.silu(gate) * up) @ w_out


def t5_bucket(n):
    n = jnp.maximum(n, 0)
    max_exact = N_BUCKETS // 2
    nf = jnp.maximum(n, 1).astype(jnp.float32)
    large = max_exact + (jnp.log(nf / max_exact) / math.log(MAX_DISTANCE / max_exact)
                         * (N_BUCKETS - max_exact)).astype(jnp.int32)
    large = jnp.minimum(large, N_BUCKETS - 1)
    return jnp.where(n < max_exact, n, large)


def rotary(x, positions):
    d = x.shape[-1]
    inv = ROPE_BASE ** (-jnp.linspace(0.0, 1.0, d // 2, dtype=jnp.float32))
    ang = positions.astype(jnp.float32)[..., None] * inv
    cos, sin = jnp.cos(ang)[:, :, None, :], jnp.sin(ang)[:, :, None, :]
    xf = x.astype(jnp.float32)
    x1, x2 = xf[..., 0::2], xf[..., 1::2]
    out = jnp.stack([x1 * cos - x2 * sin, x1 * sin + x2 * cos], axis=-1)
    return out.reshape(x.shape).astype(x.dtype)


def diff_attention(q, k, v, positions, rel_bias, lam, lam_init, norm_g):
    bsz, seq = q.shape[:2]
    nb = seq // A_QBLOCK
    scale = A_HEAD_DIM ** -0.5
    qb = jnp.moveaxis(q.reshape(bsz, nb, A_QBLOCK, A_HEADS, 2, A_HEAD_DIM), 1, 0)
    pb = jnp.moveaxis(positions.reshape(bsz, nb, A_QBLOCK), 1, 0)
    starts = jnp.arange(nb, dtype=jnp.int32) * A_QBLOCK
    k_idx = jnp.arange(seq, dtype=jnp.int32)

    def block(args):
        qi, pi, s0 = args
        logits = jnp.einsum('bqhcd,bkhcd->bhcqk', qi, k).astype(jnp.float32) * scale
        rel = pi[:, :, None] - positions[:, None, :]
        bias = jnp.moveaxis(rel_bias.astype(jnp.float32)[t5_bucket(rel)], -1, 1)
        q_idx = s0 + jnp.arange(A_QBLOCK, dtype=jnp.int32)
        causal = q_idx[:, None] >= k_idx[None, :]
        logits = jnp.where(causal, logits + bias[:, :, None], MASK_VALUE)
        probs = jax.nn.softmax(logits, axis=-1)
        w = probs[:, :, 0] - lam * probs[:, :, 1]
        return jnp.einsum('bhqk,bkhd->bqhd', w.astype(v.dtype), v)

    o = lax.map(block, (qb, pb, starts))
    o = jnp.moveaxis(o, 0, 1).reshape(bsz, seq, A_HEADS, 2 * A_HEAD_DIM)
    o = rms_norm(o, norm_g) * (1.0 - lam_init)
    return o.reshape(bsz, seq, GROUP_WIDTH)


def retention(q, k, v, g):
    bsz, seq = q.shape[:2]
    n = seq // B_CHUNK
    log_g = jnp.log(1.0 - 2.0 ** (-5.0 - jnp.arange(B_HEADS, dtype=jnp.float32)))
    j = jnp.arange(B_CHUNK, dtype=jnp.float32)
    diff = j[:, None] - j[None, :]
    decay_mask = jnp.where(diff >= 0, jnp.exp(log_g[:, None, None] * jnp.maximum(diff, 0.0)), 0.0)
    q_dec = jnp.exp(log_g[None, :] * (j[:, None] + 1.0))
    k_dec = jnp.exp(log_g[:, None] * (B_CHUNK - 1.0 - j[None, :]))
    chunk_dec = jnp.exp(log_g * B_CHUNK)
    k = k * (B_HEAD_DIM ** -0.5)

    def to_chunks(t):
        return jnp.moveaxis(t.astype(jnp.float32).reshape(bsz, n, B_CHUNK, B_HEADS, -1), 1, 0)

    def step(state, inp):
        qc, kc, vc = inp
        scores = jnp.einsum('bthd,bshd->bhts', qc, kc) * decay_mask
        o = jnp.einsum('bhts,bshd->bthd', scores, vc)
        o = o + jnp.einsum('bthd,bhde->bthe', qc, state) * q_dec[None, :, :, None]
        state = state * chunk_dec[None, :, None, None] + jnp.einsum('bshd,bshe,hs->bhde', kc, vc, k_dec)
        return state, o

    s0 = jnp.zeros((bsz, B_HEADS, B_HEAD_DIM, B_HEAD_DIM), jnp.float32)
    _, o = lax.scan(step, s0, (to_chunks(q), to_chunks(k), to_chunks(v)))
    o = jnp.moveaxis(o, 0, 1).reshape(bsz, seq, B_HEADS, B_HEAD_DIM)
    o = head_norm(o).reshape(bsz, seq, GROUP_WIDTH).astype(v.dtype)
    return o * jax.nn.silu(g)


def spatial_gating(u, v, w_s, b_s, ln_g, ln_b):
    bsz, seq = u.shape[:2]
    n = seq // C_CHUNK
    v = layer_norm(v, ln_g, ln_b)
    vc = v.reshape(bsz, n, C_CHUNK, C_GROUPS, C_GROUP_DIM)
    mask = jnp.tril(jnp.ones((C_CHUNK, C_CHUNK), dtype=w_s.dtype))
    w = w_s * mask[None]
    mixed = jnp.einsum('gts,bnsgc->bntgc', w, vc) + b_s.T[:, :, None]
    return u * mixed.reshape(bsz, seq, GROUP_WIDTH)


def hgrn2(q, f_raw, i_in, g, lb, norm_g):
    bsz, seq = q.shape[:2]
    n = seq // D_CHUNK
    lb = lb.reshape(D_HEADS, D_EXPAND).astype(jnp.float32)
    z = f_raw.astype(jnp.float32)
    log_lb = jnp.log(jnp.maximum(lb, LB_FLOOR))
    log_f = jnp.logaddexp(jax.nn.log_sigmoid(z), log_lb + jax.nn.log_sigmoid(-z))
    key = -jnp.expm1(log_f)
    causal = jnp.tril(jnp.ones((D_CHUNK, D_CHUNK), dtype=bool))

    def to_chunks(t):
        return jnp.moveaxis(t.astype(jnp.float32).reshape(bsz, n, D_CHUNK, D_HEADS, -1), 1, 0)

    def step(state, inp):
        qc, kc, vc, lfc = inp
        b = jnp.cumsum(lfc, axis=1)
        o_inter = jnp.einsum('bthk,bhkv->bthv', qc * jnp.exp(b), state)
        rel = jnp.where(causal[None, :, :, None, None], b[:, :, None] - b[:, None, :], MASK_VALUE)
        a = jnp.einsum('bthk,bshk,btshk->bhts', qc, kc, jnp.exp(rel))
        o_intra = jnp.einsum('bhts,bshv->bthv', a, vc)
        b_last = b[:, -1]
        state = state * jnp.exp(b_last)[..., None] + jnp.einsum(
            'bshk,bshv->bhkv', kc * jnp.exp(b_last[:, None] - b), vc)
        return state, o_inter + o_intra

    s0 = jnp.zeros((bsz, D_HEADS, D_EXPAND, D_HEAD_DIM), jnp.float32)
    _, o = lax.scan(step, s0, (to_chunks(q), to_chunks(key), to_chunks(i_in), to_chunks(log_f)))
    o = jnp.moveaxis(o, 0, 1).reshape(bsz, seq, GROUP_WIDTH).astype(i_in.dtype)
    return rms_norm(o, norm_g) * jax.nn.silu(g)


def token_mixing(x, positions, layer_idx, w_in, w_out, rel_bias, diff_lambda, diff_norm_g,
                 gmlp_ln_g, gmlp_ln_b, gmlp_w_s, gmlp_b_s, lower_bound, hgrn_norm_g):
    bsz, seq = x.shape[:2]
    h = x @ w_in
    a_part, b_part, c_part, d_part = jnp.split(
        h, [A_COLS, A_COLS + B_COLS, A_COLS + B_COLS + C_COLS], axis=-1)

    qa, ka, va = jnp.split(a_part, 3, axis=-1)
    qa = qa.reshape(bsz, seq, A_HEADS, 2, A_HEAD_DIM)
    ka = ka.reshape(bsz, seq, A_HEADS, 2, A_HEAD_DIM)
    va = va.reshape(bsz, seq, A_HEADS, 2 * A_HEAD_DIM)
    lam_init = 0.8 - 0.6 * math.exp(-0.3 * layer_idx)
    lq1, lk1, lq2, lk2 = [diff_lambda[j].astype(jnp.float32) for j in range(4)]
    lam = jnp.exp(jnp.sum(lq1 * lk1)) - jnp.exp(jnp.sum(lq2 * lk2)) + lam_init
    out_a = diff_attention(qa, ka, va, positions, rel_bias, lam, lam_init, diff_norm_g)

    qb, kb, vb, gb = jnp.split(b_part, 4, axis=-1)
    qb = rotary(qb.reshape(bsz, seq, B_HEADS, B_HEAD_DIM), positions)
    kb = rotary(kb.reshape(bsz, seq, B_HEADS, B_HEAD_DIM), positions)
    vb = vb.reshape(bsz, seq, B_HEADS, B_HEAD_DIM)
    out_b = retention(qb, kb, vb, gb)

    uc, vc = jnp.split(jax.nn.gelu(c_part, approximate=False), 2, axis=-1)
    out_c = spatial_gating(uc, vc, gmlp_w_s, gmlp_b_s, gmlp_ln_g, gmlp_ln_b)

    qd, fd, idd, gd = jnp.split(d_part, 4, axis=-1)
    out_d = hgrn2(qd.reshape(bsz, seq, D_HEADS, D_EXPAND), fd.reshape(bsz, seq, D_HEADS, D_EXPAND),
                  idd.reshape(bsz, seq, D_HEADS, D_HEAD_DIM), gd, lower_bound, hgrn_norm_g)

    return jnp.concatenate([out_a, out_b, out_c, out_d], axis=-1) @ w_out


def setup_inputs(seed: int = 0) -> dict:
    key = jax.random.key(seed)
    ks = jax.random.split(key, 24)
    f32 = jnp.float32

    def nrm(k, shape, fan_in, gain=1.0):
        return jax.random.normal(k, shape, f32) * (fan_in ** -0.5) * gain

    x = jax.random.normal(ks[0], (BATCH, SEQ, D_MODEL), f32)
    p = jax.random.normal(ks[1], (DEPTH, BATCH, SEQ, PLE_DIM), f32)
    offset = jax.random.randint(ks[2], (BATCH, 1), 0, 1024, dtype=jnp.int32)
    positions = offset + jnp.arange(SEQ, dtype=jnp.int32)[None, :]
    return {
        "x": x,
        "p": p,
        "positions": positions,
        "ffn1_w_in": nrm(ks[3], (DEPTH, D_MODEL, 2 * D_FF), D_MODEL),
        "ffn1_w_out": nrm(ks[4], (DEPTH, D_FF, D_MODEL), D_FF, BETA),
        "w_mix_in": nrm(ks[5], (DEPTH, D_MODEL, IN_COLS), D_MODEL),
        "w_mix_out": nrm(ks[6], (DEPTH, MIX_WIDTH, D_MODEL), MIX_WIDTH, BETA),
        "rel_bias": 0.1 * jax.random.normal(ks[7], (N_BUCKETS, A_HEADS), f32),
        "diff_lambda": 0.1 * jax.random.normal(ks[8], (DEPTH, 4, A_HEAD_DIM), f32),
        "diff_norm_g": 1.0 + 0.02 * jax.random.normal(ks[9], (DEPTH, 2 * A_HEAD_DIM), f32),
        "gmlp_ln_g": 1.0 + 0.02 * jax.random.normal(ks[10], (DEPTH, GROUP_WIDTH), f32),
        "gmlp_ln_b": 0.02 * jax.random.normal(ks[11], (DEPTH, GROUP_WIDTH), f32),
        "gmlp_w_s": nrm(ks[12], (DEPTH, C_GROUPS, C_CHUNK, C_CHUNK), C_CHUNK),
        "gmlp_b_s": 1.0 + 0.02 * jax.random.normal(ks[13], (DEPTH, C_GROUPS, C_CHUNK), f32),
        "hgrn_lb_logits": 0.5 * jax.random.normal(ks[14], (DEPTH, D_HEADS * D_EXPAND), f32),
        "hgrn_norm_g": 1.0 + 0.02 * jax.random.normal(ks[15], (DEPTH, GROUP_WIDTH), f32),
        "ffn2_w_in": nrm(ks[16], (DEPTH, D_MODEL, 2 * D_FF), D_MODEL),
        "ffn2_w_out": nrm(ks[17], (DEPTH, D_FF, D_MODEL), D_FF, BETA),
        "ple_w_gate": nrm(ks[18], (DEPTH, D_MODEL, D_MODEL), D_MODEL),
        "ple_w_proj": nrm(ks[19], (DEPTH, PLE_DIM, D_MODEL), PLE_DIM, BETA),
        "ln_g": 1.0 + 0.02 * jax.random.normal(ks[20], (DEPTH, 4, D_MODEL), f32),
        "ln_b": 0.02 * jax.random.normal(ks[21], (DEPTH, 4, D_MODEL), f32),
    }


def reference(x, p, positions, ffn1_w_in, ffn1_w_out, w_mix_in, w_mix_out, rel_bias, diff_lambda,
              diff_norm_g, gmlp_ln_g, gmlp_ln_b, gmlp_w_s, gmlp_b_s, hgrn_lb_logits, hgrn_norm_g,
              ffn2_w_in, ffn2_w_out, ple_w_gate, ple_w_proj, ln_g, ln_b):
    lb_soft = jax.nn.softmax(hgrn_lb_logits.astype(jnp.float32), axis=0)
    lower_bounds = jnp.cumsum(lb_soft, axis=0) - lb_soft[0]
    for i in range(DEPTH):
        x = layer_norm(ALPHA * x + 0.5 * swiglu(x, ffn1_w_in[i], ffn1_w_out[i]), ln_g[i, 0], ln_b[i, 0])
        mix = token_mixing(x, positions, i, w_mix_in[i], w_mix_out[i], rel_bias, diff_lambda[i],
                           diff_norm_g[i], gmlp_ln_g[i], gmlp_ln_b[i], gmlp_w_s[i], gmlp_b_s[i],
                           lower_bounds[i], hgrn_norm_g[i])
        x = layer_norm(ALPHA * x + mix, ln_g[i, 1], ln_b[i, 1])
        x = layer_norm(ALPHA * x + 0.5 * swiglu(x, ffn2_w_in[i], ffn2_w_out[i]), ln_g[i, 2], ln_b[i, 2])
        gate = jax.nn.sigmoid(x @ ple_w_gate[i])
        x = layer_norm(ALPHA * x + gate * (p[i] @ ple_w_proj[i]), ln_g[i, 3], ln_b[i, 3])
    return x
```

```python
import functools
import math

import numpy as np
import jax
import jax.numpy as jnp
from jax import lax
from jax.experimental import pallas as pl
from jax.experimental.pallas import tpu as pltpu

D_MODEL = 2048
DEPTH = 2
PLE_DIM = 256
D_FF = 5632
GROUP_WIDTH = 512

A_HEADS = 4
A_HEAD_DIM = 64
N_BUCKETS = 32
MAX_DISTANCE = 128

B_HEADS = 4
B_HEAD_DIM = 128
B_CHUNK = 128
ROPE_BASE = 10000.0

C_GROUPS = 4
C_CHUNK = 128

D_HEADS = 4
D_CHUNK = 64
D_SUB = 16

IN_COLS = 13 * GROUP_WIDTH
ALPHA = (2 * DEPTH) ** 0.25
LN_EPS = 1e-5
MASK_VALUE = -1e30
LB_FLOOR = 1e-30

LANES = 128
VMEM_LIMIT = 56 * 1024 * 1024

F32 = jnp.float32
BF16 = jnp.bfloat16


def _dot(a, b):
    return jnp.dot(a, b, preferred_element_type=F32)


def _dot_nt(a, b):
    return lax.dot_general(a, b, (((1,), (1,)), ((), ())), preferred_element_type=F32)


def _sigmoid(x):
    return 1.0 / (1.0 + jnp.exp(-x))


def _layer_norm(y, g, b):
    mu = jnp.mean(y, axis=-1, keepdims=True)
    d = y - mu
    var = jnp.mean(d * d, axis=-1, keepdims=True)
    return d * lax.rsqrt(var + LN_EPS) * g + b


def _params(*sem):
    return pltpu.CompilerParams(dimension_semantics=sem, vmem_limit_bytes=VMEM_LIMIT)


def _ffn_kernel(x_ref, wg_ref, wu_ref, wo_ref, g_ref, b_ref, o_ref, xb_ref, acc_ref):
    j = pl.program_id(1)

    @pl.when(j == 0)
    def _():
        xb_ref[...] = x_ref[...].astype(BF16)
        acc_ref[...] = jnp.zeros_like(acc_ref)

    xb = xb_ref[...]
    hg = _dot(xb, wg_ref[...])
    hu = _dot(xb, wu_ref[...])
    a = (hg * _sigmoid(hg) * hu).astype(BF16)
    acc_ref[...] += _dot(a, wo_ref[...])

    @pl.when(j == pl.num_programs(1) - 1)
    def _():
        y = ALPHA * x_ref[...] + 0.5 * acc_ref[...]
        o_ref[...] = _layer_norm(y, g_ref[...], b_ref[...])


def _ffn_ln(x, w_in, w_out, g, b, *, tm=512, tf=512):
    n, d = x.shape
    f = w_out.shape[0]
    nf = f // tf
    return pl.pallas_call(
        _ffn_kernel,
        grid=(n // tm, nf),
        in_specs=[
            pl.BlockSpec((tm, d), lambda i, j: (i, 0)),
            pl.BlockSpec((d, tf), lambda i, j: (0, j)),
            pl.BlockSpec((d, tf), lambda i, j: (0, j + nf)),
            pl.BlockSpec((tf, d), lambda i, j: (j, 0)),
            pl.BlockSpec((1, d), lambda i, j: (0, 0)),
            pl.BlockSpec((1, d), lambda i, j: (0, 0)),
        ],
        out_specs=pl.BlockSpec((tm, d), lambda i, j: (i, 0)),
        out_shape=jax.ShapeDtypeStruct((n, d), F32),
        scratch_shapes=[pltpu.VMEM((tm, d), BF16), pltpu.VMEM((tm, d), F32)],
        compiler_params=_params("parallel", "arbitrary"),
        name="ffn_ln",
    )(x, w_in, w_in, w_out, g, b)


def _in_proj_kernel(x_ref, w_ref, o_ref, xb_ref):
    @pl.when(pl.program_id(1) == 0)
    def _():
        xb_ref[...] = x_ref[...].astype(BF16)

    o_ref[...] = _dot(xb_ref[...], w_ref[...])


def _in_proj(x, w, *, tm=1024, tn=512):
    n, d = x.shape
    c = w.shape[1]
    return pl.pallas_call(
        _in_proj_kernel,
        grid=(n // tm, c // tn),
        in_specs=[
            pl.BlockSpec((tm, d), lambda i, j: (i, 0)),
            pl.BlockSpec((d, tn), lambda i, j: (0, j)),
        ],
        out_specs=pl.BlockSpec((tm, tn), lambda i, j: (i, j)),
        out_shape=jax.ShapeDtypeStruct((n, c), F32),
        scratch_shapes=[pltpu.VMEM((tm, d), BF16)],
        compiler_params=_params("parallel", "arbitrary"),
        name="in_proj",
    )(x, w)


def _bias_tiles_kernel(rb_ref, o_ref, *, t):
    h = pl.program_id(0)
    i = lax.broadcasted_iota(jnp.int32, (t, t), 0)
    j = lax.broadcasted_iota(jnp.int32, (t, t), 1)
    max_exact = N_BUCKETS // 2
    for sel in range(2):
        n = jnp.maximum(i - j + sel * t, 0)
        nf = jnp.maximum(n, 1).astype(F32)
        large = max_exact + (jnp.log(nf / max_exact) / math.log(MAX_DISTANCE / max_exact)
                             * (N_BUCKETS - max_exact)).astype(jnp.int32)
        large = jnp.minimum(large, N_BUCKETS - 1)
        bucket = jnp.where(n < max_exact, n, large)
        val = jnp.zeros((t, t), F32)
        for bkt in range(N_BUCKETS):
            val = jnp.where(bucket == bkt, rb_ref[bkt, h], val)
        if sel == 0:
            val = jnp.where(i >= j, val, MASK_VALUE)
        o_ref[0, sel] = val


def _bias_tiles(rel_bias, t):
    return pl.pallas_call(
        functools.partial(_bias_tiles_kernel, t=t),
        grid=(A_HEADS,),
        in_specs=[pl.BlockSpec(memory_space=pltpu.SMEM)],
        out_specs=pl.BlockSpec((1, 2, t, t), lambda h: (h, 0, 0, 0)),
        out_shape=jax.ShapeDtypeStruct((A_HEADS, 2, t, t), F32),
        compiler_params=_params("arbitrary"),
        name="t5_bias_tiles",
    )(rel_bias)


def _attn_kernel(q_ref, k_ref, v_ref, bt_ref, rb_ref, lam_ref, ng_ref, o_ref,
                 kb_ref, vb_ref, qm_ref, m_ref, l_ref, acc_ref, *, t, lam_init):
    h = pl.program_id(1)
    i = pl.program_id(2)

    @pl.when(i == 0)
    def _():
        kb_ref[...] = k_ref[...].astype(BF16)
        vb_ref[...] = v_ref[...].astype(BF16)

    q = q_ref[...] * (A_HEAD_DIM ** -0.5)
    lane = lax.broadcasted_iota(jnp.int32, q.shape, 1)
    qm_ref[0] = jnp.where(lane < A_HEAD_DIM, q, 0.0).astype(BF16)
    qm_ref[1] = jnp.where(lane >= A_HEAD_DIM, q, 0.0).astype(BF16)
    m_ref[...] = jnp.full(m_ref.shape, MASK_VALUE, F32)
    l_ref[...] = jnp.zeros_like(l_ref)
    acc_ref[...] = jnp.zeros_like(acc_ref)

    def step(start, bias):
        kt = kb_ref[pl.ds(start, t), :]
        vt = vb_ref[pl.ds(start, t), :]
        for c in range(2):
            s = _dot_nt(qm_ref[c], kt) + bias
            m_old = m_ref[c]
            m_new = jnp.maximum(m_old, jnp.max(s, axis=1, keepdims=True))
            p = jnp.exp(s - m_new)
            corr = jnp.exp(m_old - m_new)
            l_ref[c] = corr * l_ref[c] + jnp.sum(p, axis=1, keepdims=True)
            acc_ref[c] = corr * acc_ref[c] + _dot(p.astype(BF16), vt)
            m_ref[c] = m_new

    far_bias = rb_ref[N_BUCKETS - 1, h]

    def far_body(kj, carry):
        step(pl.multiple_of(kj * t, t), far_bias)
        return carry

    lax.fori_loop(0, i - 1, far_body, 0)

    @pl.when(i >= 1)
    def _():
        step(pl.multiple_of((i - 1) * t, t), bt_ref[0, 1])

    step(pl.multiple_of(i * t, t), bt_ref[0, 0])

    lmb = lam_ref[...]
    lam = (jnp.exp(jnp.sum(lmb[0:1] * lmb[1:2], axis=1, keepdims=True))
           - jnp.exp(jnp.sum(lmb[2:3] * lmb[3:4], axis=1, keepdims=True)) + lam_init)
    o = acc_ref[0] / l_ref[0] - lam * (acc_ref[1] / l_ref[1])
    o = o * lax.rsqrt(jnp.mean(o * o, axis=-1, keepdims=True) + LN_EPS) * ng_ref[...]
    o_ref[...] = (o * (1.0 - lam_init)).astype(o_ref.dtype)


def _diff_attention(h, bias_tiles, rel_bias, diff_lambda, norm_g, *, bsz, seq, t, layer_idx):
    n = bsz * seq
    nq = seq // t
    lam_init = 0.8 - 0.6 * math.exp(-0.3 * layer_idx)
    kcol = GROUP_WIDTH // LANES
    return pl.pallas_call(
        functools.partial(_attn_kernel, t=t, lam_init=lam_init),
        grid=(bsz, A_HEADS, nq),
        in_specs=[
            pl.BlockSpec((t, LANES), lambda b, hh, i: (b * nq + i, hh)),
            pl.BlockSpec((seq, LANES), lambda b, hh, i: (b, kcol + hh)),
            pl.BlockSpec((seq, LANES), lambda b, hh, i: (b, 2 * kcol + hh)),
            pl.BlockSpec((1, 2, t, t), lambda b, hh, i: (hh, 0, 0, 0)),
            pl.BlockSpec(memory_space=pltpu.SMEM),
            pl.BlockSpec((4, A_HEAD_DIM), lambda b, hh, i: (0, 0)),
            pl.BlockSpec((1, LANES), lambda b, hh, i: (0, 0)),
        ],
        out_specs=pl.BlockSpec((t, LANES), lambda b, hh, i: (b * nq + i, hh)),
        out_shape=jax.ShapeDtypeStruct((n, GROUP_WIDTH), BF16),
        scratch_shapes=[
            pltpu.VMEM((seq, LANES), BF16), pltpu.VMEM((seq, LANES), BF16),
            pltpu.VMEM((2, t, LANES), BF16),
            pltpu.VMEM((2, t, 1), F32), pltpu.VMEM((2, t, 1), F32), pltpu.VMEM((2, t, LANES), F32),
        ],
        compiler_params=_params("parallel", "parallel", "arbitrary"),
        name="diff_attention",
    )(h, h, h, bias_tiles, rel_bias, diff_lambda, norm_g)


def _rope_kernel(pos_ref, inv_ref, cos_ref, sin_ref):
    ang = pos_ref[...].astype(F32) * inv_ref[...]
    lane = lax.broadcasted_iota(jnp.int32, ang.shape, 1)
    cos_ref[...] = jnp.cos(ang)
    sn = jnp.sin(ang)
    sin_ref[...] = jnp.where(lane < B_HEAD_DIM // 2, -sn, sn)


def _rope_tables(pos_b, inv_b, *, tr=1024):
    n = pos_b.shape[0]
    return pl.pallas_call(
        _rope_kernel,
        grid=(n // tr,),
        in_specs=[pl.BlockSpec((tr, LANES), lambda i: (i, 0)), pl.BlockSpec((1, LANES), lambda i: (0, 0))],
        out_specs=[pl.BlockSpec((tr, LANES), lambda i: (i, 0))] * 2,
        out_shape=[jax.ShapeDtypeStruct((n, LANES), F32)] * 2,
        compiler_params=_params("parallel"),
        name="rope_tables",
    )(pos_b, inv_b)


def _retention_consts():
    log_g = np.log(1.0 - 2.0 ** (-5.0 - np.arange(B_HEADS, dtype=np.float64)))
    j = np.arange(B_CHUNK, dtype=np.float64)
    diff = j[:, None] - j[None, :]
    decay_mask = np.where(diff >= 0, np.exp(log_g[:, None, None] * np.maximum(diff, 0.0)), 0.0)
    q_dec = np.exp(log_g[:, None] * (j[None, :] + 1.0))
    k_dec = np.exp(log_g[:, None] * (B_CHUNK - 1.0 - j[None, :]))
    chunk_dec = np.exp(log_g * B_CHUNK)
    bc = lambda a: np.ascontiguousarray(np.broadcast_to(a[:, :, None], (B_HEADS, B_CHUNK, LANES)))
    return (decay_mask.astype(np.float32), bc(q_dec).astype(np.float32), bc(k_dec).astype(np.float32),
            [float(c) for c in chunk_dec])


def _retention_kernel(q_ref, k_ref, v_ref, g_ref, cos_ref, sin_ref, dm_ref, qd_ref, kd_ref, o_ref, st_ref,
                      *, chunk_dec):
    @pl.when(pl.program_id(1) == 0)
    def _():
        st_ref[...] = jnp.zeros_like(st_ref)

    tr = q_ref.shape[0]
    half = B_HEAD_DIM // 2
    for c in range(tr // B_CHUNK):
        rows = slice(c * B_CHUNK, (c + 1) * B_CHUNK)
        cos = cos_ref[rows, :]
        sin = sin_ref[rows, :]
        for hh in range(B_HEADS):
            cols = slice(hh * B_HEAD_DIM, (hh + 1) * B_HEAD_DIM)
            q = q_ref[rows, cols]
            k = k_ref[rows, cols]
            q = q * cos + pltpu.roll(q, half, 1) * sin
            k = (k * cos + pltpu.roll(k, half, 1) * sin) * (B_HEAD_DIM ** -0.5)
            qb = q.astype(BF16)
            vb = v_ref[rows, cols].astype(BF16)
            scores = _dot_nt(qb, k.astype(BF16)) * dm_ref[hh]
            st = st_ref[hh]
            o = _dot(scores.astype(BF16), vb) + _dot(qb, st.astype(BF16)) * qd_ref[hh]
            st_ref[hh] = st * chunk_dec[hh] + _dot((k * kd_ref[hh]).T.astype(BF16), vb)
            mu = jnp.mean(o, axis=-1, keepdims=True)
            d = o - mu
            var = jnp.mean(d * d, axis=-1, keepdims=True)
            gate = g_ref[rows, cols]
            o_ref[rows, cols] = (d * lax.rsqrt(var + LN_EPS) * (gate * _sigmoid(gate))).astype(o_ref.dtype)


def _retention(h, cos_t, sin_t, *, bsz, seq, tr=512):
    n = bsz * seq
    ns = seq // tr
    dm, qd, kd, chunk_dec = _retention_consts()
    row = lambda b, s: b * ns + s
    const3 = pl.BlockSpec((B_HEADS, B_CHUNK, LANES), lambda b, s: (0, 0, 0))
    return pl.pallas_call(
        functools.partial(_retention_kernel, chunk_dec=chunk_dec),
        grid=(bsz, ns),
        in_specs=[
            pl.BlockSpec((tr, GROUP_WIDTH), lambda b, s: (row(b, s), 3)),
            pl.BlockSpec((tr, GROUP_WIDTH), lambda b, s: (row(b, s), 4)),
            pl.BlockSpec((tr, GROUP_WIDTH), lambda b, s: (row(b, s), 5)),
            pl.BlockSpec((tr, GROUP_WIDTH), lambda b, s: (row(b, s), 6)),
            pl.BlockSpec((tr, LANES), lambda b, s: (row(b, s), 0)),
            pl.BlockSpec((tr, LANES), lambda b, s: (row(b, s), 0)),
            const3, const3, const3,
        ],
        out_specs=pl.BlockSpec((tr, GROUP_WIDTH), lambda b, s: (row(b, s), 0)),
        out_shape=jax.ShapeDtypeStruct((n, GROUP_WIDTH), BF16),
        scratch_shapes=[pltpu.VMEM((B_HEADS, B_HEAD_DIM, B_HEAD_DIM), F32)],
        compiler_params=_params("parallel", "arbitrary"),
        name="retention",
    )(h, h, h, h, cos_t, sin_t, jnp.asarray(dm), jnp.asarray(qd), jnp.asarray(kd))


def _gelu(x):
    return 0.5 * x * (1.0 + lax.erf(x * (0.5 ** 0.5)))


def _gmlp_kernel(u_ref, v_ref, g_ref, b_ref, ws_ref, bs_ref, o_ref):
    tr = u_ref.shape[0]
    v = _layer_norm(_gelu(v_ref[...]), g_ref[...], b_ref[...]).astype(BF16)
    ti = lax.broadcasted_iota(jnp.int32, (C_CHUNK, C_CHUNK), 0)
    si = lax.broadcasted_iota(jnp.int32, (C_CHUNK, C_CHUNK), 1)
    for gi in range(C_GROUPS):
        cols = slice(gi * LANES, (gi + 1) * LANES)
        w = jnp.where(ti >= si, ws_ref[gi], 0.0).astype(BF16)
        for c in range(tr // C_CHUNK):
            rows = slice(c * C_CHUNK, (c + 1) * C_CHUNK)
            mixed = _dot(w, v[rows, cols]) + bs_ref[gi]
            o_ref[rows, cols] = (_gelu(u_ref[rows, cols]) * mixed).astype(o_ref.dtype)


def _spatial_gating(h, ln_g, ln_b, w_s, bs_b, *, tr=512):
    n = h.shape[0]
    const3 = pl.BlockSpec((C_GROUPS, C_CHUNK, LANES), lambda i: (0, 0, 0))
    vec = pl.BlockSpec((1, GROUP_WIDTH), lambda i: (0, 0))
    return pl.pallas_call(
        _gmlp_kernel,
        grid=(n // tr,),
        in_specs=[
            pl.BlockSpec((tr, GROUP_WIDTH), lambda i: (i, 7)),
            pl.BlockSpec((tr, GROUP_WIDTH), lambda i: (i, 8)),
            vec, vec, const3, const3,
        ],
        out_specs=pl.BlockSpec((tr, GROUP_WIDTH), lambda i: (i, 0)),
        out_shape=jax.ShapeDtypeStruct((n, GROUP_WIDTH), BF16),
        compiler_params=_params("parallel"),
        name="spatial_gating",
    )(h, h, ln_g, ln_b, w_s, bs_b)


def _hgrn_kernel(q_ref, z_ref, i_ref, g_ref, lbl_ref, ng_ref, o_ref, st_ref, oc_ref, *, layer_idx):
    @pl.when(pl.program_id(1) == 0)
    def _():
        st_ref[...] = jnp.zeros_like(st_ref)

    tr = q_ref.shape[0]
    cs, sub = D_CHUNK, D_SUB

    logits = lbl_ref[...]
    e = jnp.exp(logits - jnp.max(logits, axis=0, keepdims=True))
    soft = e / jnp.sum(e, axis=0, keepdims=True)
    lb = jnp.sum(soft[0:layer_idx + 1], axis=0, keepdims=True) - soft[0:1]
    lb = jnp.maximum(lb, LB_FLOOR)

    ti = lax.broadcasted_iota(jnp.int32, (cs, cs), 0)
    si = lax.broadcasted_iota(jnp.int32, (cs, cs), 1)
    tri = jnp.where(ti >= si, 1.0, 0.0).astype(F32)
    ones = jnp.ones((LANES, LANES), BF16)
    row_in_sub = lax.broadcasted_iota(jnp.int32, (cs, LANES), 0) % sub

    for c in range(tr // cs):
        rows = slice(c * cs, (c + 1) * cs)
        z = z_ref[rows, :]
        ez = jnp.exp(-jnp.abs(z))
        r = 1.0 / (1.0 + ez)
        sig_pos = jnp.where(z >= 0, r, ez * r)
        sig_neg = jnp.where(z >= 0, ez * r, r)
        log_f = jnp.log(sig_pos + lb * sig_neg)
        key_all = (1.0 - lb) * sig_neg
        b_all = jnp.dot(tri, log_f, preferred_element_type=F32, precision=lax.Precision.HIGHEST)
        for hh in range(D_HEADS):
            cols = slice(hh * LANES, (hh + 1) * LANES)
            q = q_ref[rows, cols]
            k = key_all[:, cols]
            v = i_ref[rows, cols]
            b = b_all[:, cols]
            vb = v.astype(BF16)
            st = st_ref[hh]
            o = _dot_nt((q * jnp.exp(b)).astype(BF16), st.astype(BF16))
            a0 = _dot((q * k).astype(BF16), ones)
            o = o + a0 * v
            for d in range(1, sub):
                valid = row_in_sub >= d
                kd = pltpu.roll(k, d, 0)
                bd = pltpu.roll(b, d, 0)
                vd = pltpu.roll(v, d, 0)
                p = q * kd * jnp.exp(jnp.where(valid, b - bd, MASK_VALUE))
                o = o + _dot(p.astype(BF16), ones) * vd
            slabs = [o[0:sub]]
            for si_ in range(1, cs // sub):
                lo = si_ * sub
                bref = b[lo - 1:lo, :]
                qt = (q[lo:lo + sub] * jnp.exp(b[lo:lo + sub] - bref)).astype(BF16)
                kt = (k[0:lo] * jnp.exp(bref - b[0:lo])).astype(BF16)
                a = _dot_nt(qt, kt)
                slabs.append(o[lo:lo + sub] + _dot(a.astype(BF16), vb[0:lo]))
            oc_ref[rows, cols] = jnp.concatenate(slabs, axis=0)
            b_last = b[cs - 1:cs, :]
            kdec = (k * jnp.exp(b_last - b)).astype(BF16)
            st_ref[hh] = st * jnp.exp(b_last) + _dot(v.T.astype(BF16), kdec)

    o = oc_ref[...]
    o = o * lax.rsqrt(jnp.mean(o * o, axis=-1, keepdims=True) + LN_EPS) * ng_ref[...]
    gate = g_ref[...]
    o_ref[...] = (o * (gate * _sigmoid(gate))).astype(o_ref.dtype)


def _hgrn2(h, lb_logits, norm_g, *, bsz, seq, layer_idx, tr=256):
    n = bsz * seq
    ns = seq // tr
    row = lambda b, s: b * ns + s
    return pl.pallas_call(
        functools.partial(_hgrn_kernel, layer_idx=layer_idx),
        grid=(bsz, ns),
        in_specs=[
            pl.BlockSpec((tr, GROUP_WIDTH), lambda b, s: (row(b, s), 9)),
            pl.BlockSpec((tr, GROUP_WIDTH), lambda b, s: (row(b, s), 10)),
            pl.BlockSpec((tr, GROUP_WIDTH), lambda b, s: (row(b, s), 11)),
            pl.BlockSpec((tr, GROUP_WIDTH), lambda b, s: (row(b, s), 12)),
            pl.BlockSpec((DEPTH, GROUP_WIDTH), lambda b, s: (0, 0)),
            pl.BlockSpec((1, GROUP_WIDTH), lambda b, s: (0, 0)),
        ],
        out_specs=pl.BlockSpec((tr, GROUP_WIDTH), lambda b, s: (row(b, s), 0)),
        out_shape=jax.ShapeDtypeStruct((n, GROUP_WIDTH), BF16),
        scratch_shapes=[pltpu.VMEM((D_HEADS, LANES, LANES), F32), pltpu.VMEM((tr, GROUP_WIDTH), F32)],
        compiler_params=_params("parallel", "arbitrary"),
        name="hgrn2",
    )(h, h, h, h, lb_logits, norm_g)


def _out_proj_kernel(x_ref, a_ref, b_ref, c_ref, d_ref, w_ref, g_ref, bb_ref, o_ref):
    acc = _dot(a_ref[...], w_ref[0])
    acc += _dot(b_ref[...], w_ref[1])
    acc += _dot(c_ref[...], w_ref[2])
    acc += _dot(d_ref[...], w_ref[3])
    o_ref[...] = _layer_norm(ALPHA * x_ref[...] + acc, g_ref[...], bb_ref[...])


def _out_proj_ln(x, parts, w, g, b, *, tm=512):
    n, d = x.shape
    part = pl.BlockSpec((tm, GROUP_WIDTH), lambda i: (i, 0))
    vec = pl.BlockSpec((1, d), lambda i: (0, 0))
    return pl.pallas_call(
        _out_proj_kernel,
        grid=(n // tm,),
        in_specs=[pl.BlockSpec((tm, d), lambda i: (i, 0)), part, part, part, part,
                  pl.BlockSpec((4, GROUP_WIDTH, d), lambda i: (0, 0, 0)), vec, vec],
        out_specs=pl.BlockSpec((tm, d), lambda i: (i, 0)),
        out_shape=jax.ShapeDtypeStruct((n, d), F32),
        compiler_params=_params("parallel"),
        name="out_proj_ln",
    )(x, *parts, w, g, b)


def _ple_kernel(x_ref, p_ref, wg_ref, we_ref, g_ref, b_ref, o_ref):
    x = x_ref[...]
    gate = _sigmoid(_dot(x.astype(BF16), wg_ref[...]))
    emb = _dot(p_ref[...].astype(BF16), we_ref[...])
    o_ref[...] = _layer_norm(ALPHA * x + gate * emb, g_ref[...], b_ref[...])


def _ple_ln(x, p, wg, we, g, b, *, tm=512):
    n, d = x.shape
    vec = pl.BlockSpec((1, d), lambda i: (0, 0))
    return pl.pallas_call(
        _ple_kernel,
        grid=(n // tm,),
        in_specs=[pl.BlockSpec((tm, d), lambda i: (i, 0)), pl.BlockSpec((tm, PLE_DIM), lambda i: (i, 0)),
                  pl.BlockSpec((d, d), lambda i: (0, 0)), pl.BlockSpec((PLE_DIM, d), lambda i: (0, 0)), vec, vec],
        out_specs=pl.BlockSpec((tm, d), lambda i: (i, 0)),
        out_shape=jax.ShapeDtypeStruct((n, d), F32),
        compiler_params=_params("parallel"),
        name="ple_ln",
    )(x, p, wg, we, g, b)


def _rotary_column_order():
    per_head = np.concatenate([np.arange(0, B_HEAD_DIM, 2), np.arange(1, B_HEAD_DIM, 2)])
    return np.concatenate([hh * B_HEAD_DIM + per_head for hh in range(B_HEADS)])


def _mix_in_weight(w):
    perm = _rotary_column_order()
    cols = np.arange(IN_COLS)
    for blk in (3, 4):
        cols[blk * GROUP_WIDTH:(blk + 1) * GROUP_WIDTH] = blk * GROUP_WIDTH + perm
    return w[:, cols].astype(BF16)


def kernel(x, p, positions, ffn1_w_in, ffn1_w_out, w_mix_in, w_mix_out, rel_bias, diff_lambda, diff_norm_g,
           gmlp_ln_g, gmlp_ln_b, gmlp_w_s, gmlp_b_s, hgrn_lb_logits, hgrn_norm_g, ffn2_w_in, ffn2_w_out,
           ple_w_gate, ple_w_proj, ln_g, ln_b):
    bsz, seq, d = x.shape
    n = bsz * seq
    t_attn = min(512, seq)
    xs = x.reshape(n, d)

    pos_b = jnp.broadcast_to(positions.reshape(n, 1), (n, LANES))
    inv = ROPE_BASE ** (-jnp.linspace(0.0, 1.0, B_HEAD_DIM // 2, dtype=F32))
    inv_b = jnp.concatenate([inv, inv]).reshape(1, LANES)
    cos_t, sin_t = _rope_tables(pos_b, inv_b, tr=min(1024, n))
    bias_tiles = _bias_tiles(rel_bias, t_attn)

    for i in range(DEPTH):
        lng = ln_g[i].reshape(4, 1, d)
        lnb = ln_b[i].reshape(4, 1, d)
        xs = _ffn_ln(xs, ffn1_w_in[i].astype(BF16), ffn1_w_out[i].astype(BF16), lng[0], lnb[0])
        h = _in_proj(xs, _mix_in_weight(w_mix_in[i]))
        out_a = _diff_attention(h, bias_tiles, rel_bias, diff_lambda[i], diff_norm_g[i].reshape(1, LANES),
                                bsz=bsz, seq=seq, t=t_attn, layer_idx=i)
        out_b = _retention(h, cos_t, sin_t, bsz=bsz, seq=seq)
        bs_b = jnp.broadcast_to(gmlp_b_s[i][:, :, None], (C_GROUPS, C_CHUNK, LANES))
        out_c = _spatial_gating(h, gmlp_ln_g[i].reshape(1, GROUP_WIDTH), gmlp_ln_b[i].reshape(1, GROUP_WIDTH),
                                gmlp_w_s[i], bs_b)
        out_d = _hgrn2(h, hgrn_lb_logits, hgrn_norm_g[i].reshape(1, GROUP_WIDTH), bsz=bsz, seq=seq, layer_idx=i)
        xs = _out_proj_ln(xs, (out_a, out_b, out_c, out_d),
                          w_mix_out[i].astype(BF16).reshape(4, GROUP_WIDTH, d), lng[1], lnb[1])
        xs = _ffn_ln(xs, ffn2_w_in[i].astype(BF16), ffn2_w_out[i].astype(BF16), lng[2], lnb[2])
        xs = _ple_ln(xs, p[i].reshape(n, PLE_DIM), ple_w_gate[i].astype(BF16), ple_w_proj[i].astype(BF16),
                     lng[3], lnb[3])
    return xs.reshape(bsz, seq, d)
```

```python
import functools
import math

import numpy as np
import jax
import jax.numpy as jnp
from jax import lax
from jax.experimental import pallas as pl
from jax.experimental.pallas import tpu as pltpu

D_MODEL = 2048
DEPTH = 2
PLE_DIM = 256
D_FF = 5632
GROUP_WIDTH = 512

A_HEADS = 4
A_HEAD_DIM = 64
N_BUCKETS = 32
MAX_DISTANCE = 128

B_HEADS = 4
B_HEAD_DIM = 128
B_CHUNK = 128
ROPE_BASE = 10000.0

C_GROUPS = 4
C_CHUNK = 128

D_HEADS = 4
D_CHUNK = 64
D_SUB = 16

IN_COLS = 13 * GROUP_WIDTH
ALPHA = (2 * DEPTH) ** 0.25
LN_EPS = 1e-5
MASK_VALUE = -1e30
LB_FLOOR = 1e-30
LOG2E = math.log2(math.e)

LANES = 128
VMEM_LIMIT = 56 * 1024 * 1024

F32 = jnp.float32
BF16 = jnp.bfloat16


def _dot(a, b):
    return jnp.dot(a, b, preferred_element_type=F32)


def _dot_nt(a, b):
    return lax.dot_general(a, b, (((1,), (1,)), ((), ())), preferred_element_type=F32)


def _sigmoid(x):
    return 1.0 / (1.0 + jnp.exp(-x))


def _layer_norm(y, g, b):
    mu = jnp.mean(y, axis=-1, keepdims=True)
    d = y - mu
    var = jnp.mean(d * d, axis=-1, keepdims=True)
    return d * lax.rsqrt(var + LN_EPS) * g + b


def _params(*sem):
    return pltpu.CompilerParams(dimension_semantics=sem, vmem_limit_bytes=VMEM_LIMIT)


def _ffn_kernel(x_ref, wg_ref, wu_ref, wo_ref, g_ref, b_ref, o_ref, xb_ref, acc_ref):
    j = pl.program_id(1)

    @pl.when(j == 0)
    def _():
        xb_ref[...] = x_ref[...].astype(BF16)
        acc_ref[...] = jnp.zeros_like(acc_ref)

    xb = xb_ref[...]
    hg = _dot(xb, wg_ref[...])
    hu = _dot(xb, wu_ref[...])
    a = (hg * _sigmoid(hg) * hu).astype(BF16)
    acc_ref[...] += _dot(a, wo_ref[...])

    @pl.when(j == pl.num_programs(1) - 1)
    def _():
        y = ALPHA * x_ref[...] + 0.5 * acc_ref[...]
        o_ref[...] = _layer_norm(y, g_ref[...], b_ref[...])


def _ffn_ln(x, w_in, w_out, g, b, *, tm=512, tf=512):
    n, d = x.shape
    f = w_out.shape[0]
    nf = f // tf
    return pl.pallas_call(
        _ffn_kernel,
        grid=(n // tm, nf),
        in_specs=[
            pl.BlockSpec((tm, d), lambda i, j: (i, 0)),
            pl.BlockSpec((d, tf), lambda i, j: (0, j)),
            pl.BlockSpec((d, tf), lambda i, j: (0, j + nf)),
            pl.BlockSpec((tf, d), lambda i, j: (j, 0)),
            pl.BlockSpec((1, d), lambda i, j: (0, 0)),
            pl.BlockSpec((1, d), lambda i, j: (0, 0)),
        ],
        out_specs=pl.BlockSpec((tm, d), lambda i, j: (i, 0)),
        out_shape=jax.ShapeDtypeStruct((n, d), F32),
        scratch_shapes=[pltpu.VMEM((tm, d), BF16), pltpu.VMEM((tm, d), F32)],
        compiler_params=_params("parallel", "arbitrary"),
        name="ffn_ln",
    )(x, w_in, w_in, w_out, g, b)


def _in_proj_kernel(x_ref, w_ref, o_ref, xb_ref):
    @pl.when(pl.program_id(1) == 0)
    def _():
        xb_ref[...] = x_ref[...].astype(BF16)

    o_ref[...] = _dot(xb_ref[...], w_ref[...])


def _in_proj(x, w, *, tm=1024, tn=512):
    n, d = x.shape
    c = w.shape[1]
    return pl.pallas_call(
        _in_proj_kernel,
        grid=(n // tm, c // tn),
        in_specs=[
            pl.BlockSpec((tm, d), lambda i, j: (i, 0)),
            pl.BlockSpec((d, tn), lambda i, j: (0, j)),
        ],
        out_specs=pl.BlockSpec((tm, tn), lambda i, j: (i, j)),
        out_shape=jax.ShapeDtypeStruct((n, c), F32),
        scratch_shapes=[pltpu.VMEM((tm, d), BF16)],
        compiler_params=_params("parallel", "arbitrary"),
        name="in_proj",
    )(x, w)


def _bias_tiles_kernel(rb_ref, o_ref, *, t):
    h = pl.program_id(0)
    j = lax.broadcasted_iota(jnp.int32, (t, t), 0)
    i = lax.broadcasted_iota(jnp.int32, (t, t), 1)
    max_exact = N_BUCKETS // 2
    for sel in range(2):
        n = jnp.maximum(i - j + sel * t, 0)
        nf = jnp.maximum(n, 1).astype(F32)
        large = max_exact + (jnp.log(nf / max_exact) / math.log(MAX_DISTANCE / max_exact)
                             * (N_BUCKETS - max_exact)).astype(jnp.int32)
        large = jnp.minimum(large, N_BUCKETS - 1)
        bucket = jnp.where(n < max_exact, n, large)
        val = jnp.zeros((t, t), F32)
        for bkt in range(N_BUCKETS):
            val = jnp.where(bucket == bkt, rb_ref[bkt, h] * LOG2E, val)
        if sel == 0:
            val = jnp.where(i >= j, val, MASK_VALUE)
        o_ref[0, sel] = val


def _bias_tiles(rel_bias, t):
    return pl.pallas_call(
        functools.partial(_bias_tiles_kernel, t=t),
        grid=(A_HEADS,),
        in_specs=[pl.BlockSpec(memory_space=pltpu.SMEM)],
        out_specs=pl.BlockSpec((1, 2, t, t), lambda h: (h, 0, 0, 0)),
        out_shape=jax.ShapeDtypeStruct((A_HEADS, 2, t, t), F32),
        compiler_params=_params("arbitrary"),
        name="t5_bias_tiles",
    )(rel_bias)


def _attn_kernel(q_ref, k_ref, v_ref, bt_ref, rb_ref, lam_ref, ng_ref, o_ref,
                 kb_ref, vt_ref, qst_ref, m_ref, l_ref, acc_ref, *, t, lam_init):
    h = pl.program_id(1)
    i = pl.program_id(2)

    @pl.when(i == 0)
    def _():
        def fill(c, carry):
            rows = pl.ds(pl.multiple_of(c * t, t), t)
            kb_ref[c] = k_ref[rows, :].astype(BF16)
            vt_ref[c] = v_ref[rows, :].T.astype(BF16)
            return carry

        lax.fori_loop(0, kb_ref.shape[0], fill, 0)

    qt = (q_ref[...] * (A_HEAD_DIM ** -0.5 * LOG2E)).T
    feat = lax.broadcasted_iota(jnp.int32, qt.shape, 0)
    qst_ref[:, 0:t] = jnp.where(feat < A_HEAD_DIM, qt, 0.0).astype(BF16)
    qst_ref[:, t:2 * t] = jnp.where(feat >= A_HEAD_DIM, qt, 0.0).astype(BF16)
    m_ref[...] = jnp.full(m_ref.shape, MASK_VALUE, F32)
    l_ref[...] = jnp.zeros_like(l_ref)
    acc_ref[...] = jnp.zeros_like(acc_ref)

    def step(kj, tile_sel, const_bias):
        s2 = _dot(kb_ref[kj], qst_ref[...])
        vt = vt_ref[kj]
        for c in range(2):
            s = s2[:, c * t:(c + 1) * t]
            if tile_sel is not None:
                s = s + bt_ref[0, tile_sel]
            m_old = m_ref[c]
            smax = jnp.max(s, axis=0, keepdims=True)
            if const_bias is not None:
                smax = smax + const_bias
            m_new = jnp.maximum(m_old, smax)
            shift = m_new if const_bias is None else m_new - const_bias
            p = jnp.exp2(s - shift)
            corr = jnp.exp2(m_old - m_new)
            l_ref[c] = corr * l_ref[c] + jnp.sum(p, axis=0, keepdims=True)
            acc_ref[c] = corr * acc_ref[c] + _dot(vt, p.astype(BF16))
            m_ref[c] = m_new

    far_bias = rb_ref[N_BUCKETS - 1, h] * LOG2E

    def far_body(kj, carry):
        step(kj, None, far_bias)
        return carry

    lax.fori_loop(0, i - 1, far_body, 0)

    @pl.when(i >= 1)
    def _():
        step(i - 1, 1, None)

    step(i, 0, None)

    lmb = lam_ref[...]
    lam = (jnp.exp(jnp.sum(lmb[0:1] * lmb[1:2], axis=1, keepdims=True))
           - jnp.exp(jnp.sum(lmb[2:3] * lmb[3:4], axis=1, keepdims=True)) + lam_init)
    o = (acc_ref[0] / l_ref[0] - lam * (acc_ref[1] / l_ref[1])).T
    o = o * lax.rsqrt(jnp.mean(o * o, axis=-1, keepdims=True) + LN_EPS) * ng_ref[...]
    o_ref[...] = (o * (1.0 - lam_init)).astype(o_ref.dtype)


def _diff_attention(h, bias_tiles, rel_bias, diff_lambda, norm_g, *, bsz, seq, t, layer_idx):
    n = bsz * seq
    nq = seq // t
    lam_init = 0.8 - 0.6 * math.exp(-0.3 * layer_idx)
    kcol = GROUP_WIDTH // LANES
    return pl.pallas_call(
        functools.partial(_attn_kernel, t=t, lam_init=lam_init),
        grid=(bsz, A_HEADS, nq),
        in_specs=[
            pl.BlockSpec((t, LANES), lambda b, hh, i: (b * nq + i, hh)),
            pl.BlockSpec((seq, LANES), lambda b, hh, i: (b, kcol + hh)),
            pl.BlockSpec((seq, LANES), lambda b, hh, i: (b, 2 * kcol + hh)),
            pl.BlockSpec((1, 2, t, t), lambda b, hh, i: (hh, 0, 0, 0)),
            pl.BlockSpec(memory_space=pltpu.SMEM),
            pl.BlockSpec((4, A_HEAD_DIM), lambda b, hh, i: (0, 0)),
            pl.BlockSpec((1, LANES), lambda b, hh, i: (0, 0)),
        ],
        out_specs=pl.BlockSpec((t, LANES), lambda b, hh, i: (b * nq + i, hh)),
        out_shape=jax.ShapeDtypeStruct((n, GROUP_WIDTH), BF16),
        scratch_shapes=[
            pltpu.VMEM((nq, t, LANES), BF16), pltpu.VMEM((nq, LANES, t), BF16),
            pltpu.VMEM((LANES, 2 * t), BF16),
            pltpu.VMEM((2, 1, t), F32), pltpu.VMEM((2, 1, t), F32), pltpu.VMEM((2, LANES, t), F32),
        ],
        compiler_params=_params("parallel", "parallel", "arbitrary"),
        name="diff_attention",
    )(h, h, h, bias_tiles, rel_bias, diff_lambda, norm_g)


def _rope_kernel(pos_ref, inv_ref, cos_ref, sin_ref):
    ang = pos_ref[...].astype(F32) * inv_ref[...]
    lane = lax.broadcasted_iota(jnp.int32, ang.shape, 1)
    cos_ref[...] = jnp.cos(ang)
    sn = jnp.sin(ang)
    sin_ref[...] = jnp.where(lane < B_HEAD_DIM // 2, -sn, sn)


def _rope_tables(pos_b, inv_b, *, tr=1024):
    n = pos_b.shape[0]
    return pl.pallas_call(
        _rope_kernel,
        grid=(n // tr,),
        in_specs=[pl.BlockSpec((tr, LANES), lambda i: (i, 0)), pl.BlockSpec((1, LANES), lambda i: (0, 0))],
        out_specs=[pl.BlockSpec((tr, LANES), lambda i: (i, 0))] * 2,
        out_shape=[jax.ShapeDtypeStruct((n, LANES), F32)] * 2,
        compiler_params=_params("parallel"),
        name="rope_tables",
    )(pos_b, inv_b)


def _retention_consts():
    log_g = np.log(1.0 - 2.0 ** (-5.0 - np.arange(B_HEADS, dtype=np.float64)))
    j = np.arange(B_CHUNK, dtype=np.float64)
    diff = j[:, None] - j[None, :]
    decay_mask = np.where(diff >= 0, np.exp(log_g[:, None, None] * np.maximum(diff, 0.0)), 0.0)
    q_dec = np.exp(log_g[:, None] * (j[None, :] + 1.0))
    k_dec = np.exp(log_g[:, None] * (B_CHUNK - 1.0 - j[None, :]))
    chunk_dec = np.exp(log_g * B_CHUNK)
    bc = lambda a: np.ascontiguousarray(np.broadcast_to(a[:, :, None], (B_HEADS, B_CHUNK, LANES)))
    return (decay_mask.astype(np.float32), bc(q_dec).astype(np.float32), bc(k_dec).astype(np.float32),
            [float(c) for c in chunk_dec])


def _retention_kernel(q_ref, k_ref, v_ref, g_ref, cos_ref, sin_ref, dm_ref, qd_ref, kd_ref, o_ref, st_ref,
                      *, chunk_dec):
    @pl.when(pl.program_id(1) == 0)
    def _():
        st_ref[...] = jnp.zeros_like(st_ref)

    tr = q_ref.shape[0]
    half = B_HEAD_DIM // 2
    for c in range(tr // B_CHUNK):
        rows = slice(c * B_CHUNK, (c + 1) * B_CHUNK)
        cos = cos_ref[rows, :]
        sin = sin_ref[rows, :]
        for hh in range(B_HEADS):
            cols = slice(hh * B_HEAD_DIM, (hh + 1) * B_HEAD_DIM)
            q = q_ref[rows, cols]
            k = k_ref[rows, cols]
            q = q * cos + pltpu.roll(q, half, 1) * sin
            k = (k * cos + pltpu.roll(k, half, 1) * sin) * (B_HEAD_DIM ** -0.5)
            qb = q.astype(BF16)
            vb = v_ref[rows, cols].astype(BF16)
            scores = _dot_nt(qb, k.astype(BF16)) * dm_ref[hh]
            st = st_ref[hh]
            o = _dot(scores.astype(BF16), vb) + _dot(qb, st.astype(BF16)) * qd_ref[hh]
            st_ref[hh] = st * chunk_dec[hh] + _dot((k * kd_ref[hh]).T.astype(BF16), vb)
            mu = jnp.mean(o, axis=-1, keepdims=True)
            d = o - mu
            var = jnp.mean(d * d, axis=-1, keepdims=True)
            gate = g_ref[rows, cols]
            o_ref[rows, cols] = (d * lax.rsqrt(var + LN_EPS) * (gate * _sigmoid(gate))).astype(o_ref.dtype)


def _retention(h, cos_t, sin_t, *, bsz, seq, tr=512):
    n = bsz * seq
    ns = seq // tr
    dm, qd, kd, chunk_dec = _retention_consts()
    row = lambda b, s: b * ns + s
    const3 = pl.BlockSpec((B_HEADS, B_CHUNK, LANES), lambda b, s: (0, 0, 0))
    return pl.pallas_call(
        functools.partial(_retention_kernel, chunk_dec=chunk_dec),
        grid=(bsz, ns),
        in_specs=[
            pl.BlockSpec((tr, GROUP_WIDTH), lambda b, s: (row(b, s), 3)),
            pl.BlockSpec((tr, GROUP_WIDTH), lambda b, s: (row(b, s), 4)),
            pl.BlockSpec((tr, GROUP_WIDTH), lambda b, s: (row(b, s), 5)),
            pl.BlockSpec((tr, GROUP_WIDTH), lambda b, s: (row(b, s), 6)),
            pl.BlockSpec((tr, LANES), lambda b, s: (row(b, s), 0)),
            pl.BlockSpec((tr, LANES), lambda b, s: (row(b, s), 0)),
            const3, const3, const3,
        ],
        out_specs=pl.BlockSpec((tr, GROUP_WIDTH), lambda b, s: (row(b, s), 0)),
        out_shape=jax.ShapeDtypeStruct((n, GROUP_WIDTH), BF16),
        scratch_shapes=[pltpu.VMEM((B_HEADS, B_HEAD_DIM, B_HEAD_DIM), F32)],
        compiler_params=_params("parallel", "arbitrary"),
        name="retention",
    )(h, h, h, h, cos_t, sin_t, jnp.asarray(dm), jnp.asarray(qd), jnp.asarray(kd))


def _gelu(x):
    return 0.5 * x * (1.0 + lax.erf(x * (0.5 ** 0.5)))


def _gmlp_kernel(u_ref, v_ref, g_ref, b_ref, ws_ref, bs_ref, o_ref):
    tr = u_ref.shape[0]
    v = _layer_norm(_gelu(v_ref[...]), g_ref[...], b_ref[...]).astype(BF16)
    ti = lax.broadcasted_iota(jnp.int32, (C_CHUNK, C_CHUNK), 0)
    si = lax.broadcasted_iota(jnp.int32, (C_CHUNK, C_CHUNK), 1)
    for gi in range(C_GROUPS):
        cols = slice(gi * LANES, (gi + 1) * LANES)
        w = jnp.where(ti >= si, ws_ref[gi], 0.0).astype(BF16)
        for c in range(tr // C_CHUNK):
            rows = slice(c * C_CHUNK, (c + 1) * C_CHUNK)
            mixed = _dot(w, v[rows, cols]) + bs_ref[gi]
            o_ref[rows, cols] = (_gelu(u_ref[rows, cols]) * mixed).astype(o_ref.dtype)


def _spatial_gating(h, ln_g, ln_b, w_s, bs_b, *, tr=512):
    n = h.shape[0]
    const3 = pl.BlockSpec((C_GROUPS, C_CHUNK, LANES), lambda i: (0, 0, 0))
    vec = pl.BlockSpec((1, GROUP_WIDTH), lambda i: (0, 0))
    return pl.pallas_call(
        _gmlp_kernel,
        grid=(n // tr,),
        in_specs=[
            pl.BlockSpec((tr, GROUP_WIDTH), lambda i: (i, 7)),
            pl.BlockSpec((tr, GROUP_WIDTH), lambda i: (i, 8)),
            vec, vec, const3, const3,
        ],
        out_specs=pl.BlockSpec((tr, GROUP_WIDTH), lambda i: (i, 0)),
        out_shape=jax.ShapeDtypeStruct((n, GROUP_WIDTH), BF16),
        compiler_params=_params("parallel"),
        name="spatial_gating",
    )(h, h, ln_g, ln_b, w_s, bs_b)


def _hgrn_kernel(q_ref, z_ref, i_ref, g_ref, lbl_ref, ng_ref, o_ref, st_ref, oc_ref, *, layer_idx):
    @pl.when(pl.program_id(1) == 0)
    def _():
        st_ref[...] = jnp.zeros_like(st_ref)

    tr = q_ref.shape[0]
    cs, sub = D_CHUNK, D_SUB

    logits = lbl_ref[...]
    e = jnp.exp(logits - jnp.max(logits, axis=0, keepdims=True))
    soft = e / jnp.sum(e, axis=0, keepdims=True)
    lb = jnp.sum(soft[0:layer_idx + 1], axis=0, keepdims=True) - soft[0:1]
    lb = jnp.maximum(lb, LB_FLOOR)

    ti = lax.broadcasted_iota(jnp.int32, (cs, cs), 0)
    si = lax.broadcasted_iota(jnp.int32, (cs, cs), 1)
    tri = jnp.where(ti >= si, 1.0, 0.0).astype(F32)
    ones = jnp.ones((LANES, LANES), BF16)
    row_in_sub = lax.broadcasted_iota(jnp.int32, (cs, LANES), 0) % sub

    for c in range(tr // cs):
        rows = slice(c * cs, (c + 1) * cs)
        z = z_ref[rows, :]
        ez = jnp.exp(-jnp.abs(z))
        r = 1.0 / (1.0 + ez)
        sig_pos = jnp.where(z >= 0, r, ez * r)
        sig_neg = jnp.where(z >= 0, ez * r, r)
        log_f = jnp.log(sig_pos + lb * sig_neg)
        key_all = (1.0 - lb) * sig_neg
        b_all = jnp.dot(tri, log_f, preferred_element_type=F32, precision=lax.Precision.HIGHEST)
        for hh in range(D_HEADS):
            cols = slice(hh * LANES, (hh + 1) * LANES)
            q = q_ref[rows, cols]
            k = key_all[:, cols]
            v = i_ref[rows, cols]
            b = b_all[:, cols]
            vb = v.astype(BF16)
            st = st_ref[hh]
            o = _dot_nt((q * jnp.exp(b)).astype(BF16), st.astype(BF16))
            a0 = _dot((q * k).astype(BF16), ones)
            o = o + a0 * v
            for d in range(1, sub):
                valid = row_in_sub >= d
                kd = pltpu.roll(k, d, 0)
                bd = pltpu.roll(b, d, 0)
                vd = pltpu.roll(v, d, 0)
                p = q * kd * jnp.exp(jnp.where(valid, b - bd, MASK_VALUE))
                o = o + _dot(p.astype(BF16), ones) * vd
            slabs = [o[0:sub]]
            for si_ in range(1, cs // sub):
                lo = si_ * sub
                bref = b[lo - 1:lo, :]
                qt = (q[lo:lo + sub] * jnp.exp(b[lo:lo + sub] - bref)).astype(BF16)
                kt = (k[0:lo] * jnp.exp(bref - b[0:lo])).astype(BF16)
                a = _dot_nt(qt, kt)
                slabs.append(o[lo:lo + sub] + _dot(a.astype(BF16), vb[0:lo]))
            oc_ref[rows, cols] = jnp.concatenate(slabs, axis=0)
            b_last = b[cs - 1:cs, :]
            kdec = (k * jnp.exp(b_last - b)).astype(BF16)
            st_ref[hh] = st * jnp.exp(b_last) + _dot(v.T.astype(BF16), kdec)

    o = oc_ref[...]
    o = o * lax.rsqrt(jnp.mean(o * o, axis=-1, keepdims=True) + LN_EPS) * ng_ref[...]
    gate = g_ref[...]
    o_ref[...] = (o * (gate * _sigmoid(gate))).astype(o_ref.dtype)


def _hgrn2(h, lb_logits, norm_g, *, bsz, seq, layer_idx, tr=256):
    n = bsz * seq
    ns = seq // tr
    row = lambda b, s: b * ns + s
    return pl.pallas_call(
        functools.partial(_hgrn_kernel, layer_idx=layer_idx),
        grid=(bsz, ns),
        in_specs=[
            pl.BlockSpec((tr, GROUP_WIDTH), lambda b, s: (row(b, s), 9)),
            pl.BlockSpec((tr, GROUP_WIDTH), lambda b, s: (row(b, s), 10)),
            pl.BlockSpec((tr, GROUP_WIDTH), lambda b, s: (row(b, s), 11)),
            pl.BlockSpec((tr, GROUP_WIDTH), lambda b, s: (row(b, s), 12)),
            pl.BlockSpec((DEPTH, GROUP_WIDTH), lambda b, s: (0, 0)),
            pl.BlockSpec((1, GROUP_WIDTH), lambda b, s: (0, 0)),
        ],
        out_specs=pl.BlockSpec((tr, GROUP_WIDTH), lambda b, s: (row(b, s), 0)),
        out_shape=jax.ShapeDtypeStruct((n, GROUP_WIDTH), BF16),
        scratch_shapes=[pltpu.VMEM((D_HEADS, LANES, LANES), F32), pltpu.VMEM((tr, GROUP_WIDTH), F32)],
        compiler_params=_params("parallel", "arbitrary"),
        name="hgrn2",
    )(h, h, h, h, lb_logits, norm_g)


def _out_proj_kernel(x_ref, a_ref, b_ref, c_ref, d_ref, w_ref, g_ref, bb_ref, o_ref):
    acc = _dot(a_ref[...], w_ref[0])
    acc += _dot(b_ref[...], w_ref[1])
    acc += _dot(c_ref[...], w_ref[2])
    acc += _dot(d_ref[...], w_ref[3])
    o_ref[...] = _layer_norm(ALPHA * x_ref[...] + acc, g_ref[...], bb_ref[...])


def _out_proj_ln(x, parts, w, g, b, *, tm=512):
    n, d = x.shape
    part = pl.BlockSpec((tm, GROUP_WIDTH), lambda i: (i, 0))
    vec = pl.BlockSpec((1, d), lambda i: (0, 0))
    return pl.pallas_call(
        _out_proj_kernel,
        grid=(n // tm,),
        in_specs=[pl.BlockSpec((tm, d), lambda i: (i, 0)), part, part, part, part,
                  pl.BlockSpec((4, GROUP_WIDTH, d), lambda i: (0, 0, 0)), vec, vec],
        out_specs=pl.BlockSpec((tm, d), lambda i: (i, 0)),
        out_shape=jax.ShapeDtypeStruct((n, d), F32),
        compiler_params=_params("parallel"),
        name="out_proj_ln",
    )(x, *parts, w, g, b)


def _ple_kernel(x_ref, p_ref, wg_ref, we_ref, g_ref, b_ref, o_ref):
    x = x_ref[...]
    gate = _sigmoid(_dot(x.astype(BF16), wg_ref[...]))
    emb = _dot(p_ref[...].astype(BF16), we_ref[...])
    o_ref[...] = _layer_norm(ALPHA * x + gate * emb, g_ref[...], b_ref[...])


def _ple_ln(x, p, wg, we, g, b, *, tm=512):
    n, d = x.shape
    vec = pl.BlockSpec((1, d), lambda i: (0, 0))
    return pl.pallas_call(
        _ple_kernel,
        grid=(n // tm,),
        in_specs=[pl.BlockSpec((tm, d), lambda i: (i, 0)), pl.BlockSpec((tm, PLE_DIM), lambda i: (i, 0)),
                  pl.BlockSpec((d, d), lambda i: (0, 0)), pl.BlockSpec((PLE_DIM, d), lambda i: (0, 0)), vec, vec],
        out_specs=pl.BlockSpec((tm, d), lambda i: (i, 0)),
        out_shape=jax.ShapeDtypeStruct((n, d), F32),
        compiler_params=_params("parallel"),
        name="ple_ln",
    )(x, p, wg, we, g, b)


def _rotary_column_order():
    per_head = np.concatenate([np.arange(0, B_HEAD_DIM, 2), np.arange(1, B_HEAD_DIM, 2)])
    return np.concatenate([hh * B_HEAD_DIM + per_head for hh in range(B_HEADS)])


def _mix_in_weight(w):
    perm = _rotary_column_order()
    cols = np.arange(IN_COLS)
    for blk in (3, 4):
        cols[blk * GROUP_WIDTH:(blk + 1) * GROUP_WIDTH] = blk * GROUP_WIDTH + perm
    return w[:, cols].astype(BF16)


def kernel(x, p, positions, ffn1_w_in, ffn1_w_out, w_mix_in, w_mix_out, rel_bias, diff_lambda, diff_norm_g,
           gmlp_ln_g, gmlp_ln_b, gmlp_w_s, gmlp_b_s, hgrn_lb_logits, hgrn_norm_g, ffn2_w_in, ffn2_w_out,
           ple_w_gate, ple_w_proj, ln_g, ln_b):
    bsz, seq, d = x.shape
    n = bsz * seq
    t_attn = min(512, seq)
    xs = x.reshape(n, d)

    pos_b = jnp.broadcast_to(positions.reshape(n, 1), (n, LANES))
    inv = ROPE_BASE ** (-jnp.linspace(0.0, 1.0, B_HEAD_DIM // 2, dtype=F32))
    inv_b = jnp.concatenate([inv, inv]).reshape(1, LANES)
    cos_t, sin_t = _rope_tables(pos_b, inv_b, tr=min(1024, n))
    bias_tiles = _bias_tiles(rel_bias, t_attn)

    for i in range(DEPTH):
        lng = ln_g[i].reshape(4, 1, d)
        lnb = ln_b[i].reshape(4, 1, d)
        xs = _ffn_ln(xs, ffn1_w_in[i].astype(BF16), ffn1_w_out[i].astype(BF16), lng[0], lnb[0])
        h = _in_proj(xs, _mix_in_weight(w_mix_in[i]))
        out_a = _diff_attention(h, bias_tiles, rel_bias, diff_lambda[i], diff_norm_g[i].reshape(1, LANES),
                                bsz=bsz, seq=seq, t=t_attn, layer_idx=i)
        out_b = _retention(h, cos_t, sin_t, bsz=bsz, seq=seq)
        bs_b = jnp.broadcast_to(gmlp_b_s[i][:, :, None], (C_GROUPS, C_CHUNK, LANES))
        out_c = _spatial_gating(h, gmlp_ln_g[i].reshape(1, GROUP_WIDTH), gmlp_ln_b[i].reshape(1, GROUP_WIDTH),
                                gmlp_w_s[i], bs_b)
        out_d = _hgrn2(h, hgrn_lb_logits, hgrn_norm_g[i].reshape(1, GROUP_WIDTH), bsz=bsz, seq=seq, layer_idx=i)
        xs = _out_proj_ln(xs, (out_a, out_b, out_c, out_d),
                          w_mix_out[i].astype(BF16).reshape(4, GROUP_WIDTH, d), lng[1], lnb[1])
        xs = _ffn_ln(xs, ffn2_w_in[i].astype(BF16), ffn2_w_out[i].astype(BF16), lng[2], lnb[2])
        xs = _ple_ln(xs, p[i].reshape(n, PLE_DIM), ple_w_gate[i].astype(BF16), ple_w_proj[i].astype(BF16),
                     lng[3], lnb[3])
    return xs.reshape(bsz, seq, d)
```

```python
import functools
import math

import numpy as np
import jax
import jax.numpy as jnp
from jax import lax
from jax.experimental import pallas as pl
from jax.experimental.pallas import tpu as pltpu

D_MODEL = 2048
DEPTH = 2
PLE_DIM = 256
D_FF = 5632
GROUP_WIDTH = 512

A_HEADS = 4
A_HEAD_DIM = 64
N_BUCKETS = 32
MAX_DISTANCE = 128

B_HEADS = 4
B_HEAD_DIM = 128
B_CHUNK = 128
ROPE_BASE = 10000.0

C_GROUPS = 4
C_CHUNK = 128

D_HEADS = 4
D_CHUNK = 64
D_SUB = 16

IN_COLS = 13 * GROUP_WIDTH
ALPHA = (2 * DEPTH) ** 0.25
LN_EPS = 1e-5
MASK_VALUE = -1e30
LB_FLOOR = 1e-30
LOG2E = math.log2(math.e)

LANES = 128
VMEM_LIMIT = 56 * 1024 * 1024

F32 = jnp.float32
BF16 = jnp.bfloat16


def _dot(a, b):
    return jnp.dot(a, b, preferred_element_type=F32)


def _dot_nt(a, b):
    return lax.dot_general(a, b, (((1,), (1,)), ((), ())), preferred_element_type=F32)


def _sigmoid(x):
    return 1.0 / (1.0 + jnp.exp(-x))


def _layer_norm(y, g, b):
    mu = jnp.mean(y, axis=-1, keepdims=True)
    d = y - mu
    var = jnp.mean(d * d, axis=-1, keepdims=True)
    return d * lax.rsqrt(var + LN_EPS) * g + b


def _params(*sem):
    return pltpu.CompilerParams(dimension_semantics=sem, vmem_limit_bytes=VMEM_LIMIT)


def _ffn_kernel(x_ref, wg_ref, wu_ref, wo_ref, g_ref, b_ref, o_ref, xb_ref):
    j = pl.program_id(1)

    @pl.when(j == 0)
    def _():
        xb_ref[...] = x_ref[...].astype(BF16)
        o_ref[...] = jnp.zeros_like(o_ref)

    xb = xb_ref[...]
    hg = _dot(xb, wg_ref[...])
    hu = _dot(xb, wu_ref[...])
    a = (hg * _sigmoid(hg) * hu).astype(BF16)
    o_ref[...] += _dot(a, wo_ref[...])

    @pl.when(j == pl.num_programs(1) - 1)
    def _():
        y = ALPHA * x_ref[...] + 0.5 * o_ref[...]
        o_ref[...] = _layer_norm(y, g_ref[...], b_ref[...])


def _ffn_ln(x, w_in, w_out, g, b, *, tm=512, tf=512):
    n, d = x.shape
    f = w_out.shape[0]
    nf = f // tf
    return pl.pallas_call(
        _ffn_kernel,
        grid=(n // tm, nf),
        in_specs=[
            pl.BlockSpec((tm, d), lambda i, j: (i, 0)),
            pl.BlockSpec((d, tf), lambda i, j: (0, j)),
            pl.BlockSpec((d, tf), lambda i, j: (0, j + nf)),
            pl.BlockSpec((tf, d), lambda i, j: (j, 0)),
            pl.BlockSpec((1, d), lambda i, j: (0, 0)),
            pl.BlockSpec((1, d), lambda i, j: (0, 0)),
        ],
        out_specs=pl.BlockSpec((tm, d), lambda i, j: (i, 0)),
        out_shape=jax.ShapeDtypeStruct((n, d), F32),
        scratch_shapes=[pltpu.VMEM((tm, d), BF16)],
        compiler_params=_params("parallel", "arbitrary"),
        name="ffn_ln",
    )(x, w_in, w_in, w_out, g, b)


def _in_proj_kernel(x_ref, w_ref, o_ref, xb_ref):
    @pl.when(pl.program_id(1) == 0)
    def _():
        xb_ref[...] = x_ref[...].astype(BF16)

    o_ref[...] = _dot(xb_ref[...], w_ref[...])


def _in_proj(x, w, *, tm=1024, tn=512):
    n, d = x.shape
    c = w.shape[1]
    return pl.pallas_call(
        _in_proj_kernel,
        grid=(n // tm, c // tn),
        in_specs=[
            pl.BlockSpec((tm, d), lambda i, j: (i, 0)),
            pl.BlockSpec((d, tn), lambda i, j: (0, j)),
        ],
        out_specs=pl.BlockSpec((tm, tn), lambda i, j: (i, j)),
        out_shape=jax.ShapeDtypeStruct((n, c), F32),
        scratch_shapes=[pltpu.VMEM((tm, d), BF16)],
        compiler_params=_params("parallel", "arbitrary"),
        name="in_proj",
    )(x, w)


def _bias_tiles_kernel(rb_ref, o_ref, *, t):
    h = pl.program_id(0)
    j = lax.broadcasted_iota(jnp.int32, (t, t), 0)
    i = lax.broadcasted_iota(jnp.int32, (t, t), 1)
    max_exact = N_BUCKETS // 2
    for sel in range(2):
        n = jnp.maximum(i - j + sel * t, 0)
        nf = jnp.maximum(n, 1).astype(F32)
        large = max_exact + (jnp.log(nf / max_exact) / math.log(MAX_DISTANCE / max_exact)
                             * (N_BUCKETS - max_exact)).astype(jnp.int32)
        large = jnp.minimum(large, N_BUCKETS - 1)
        bucket = jnp.where(n < max_exact, n, large)
        val = jnp.zeros((t, t), F32)
        for bkt in range(N_BUCKETS):
            val = jnp.where(bucket == bkt, rb_ref[bkt, h] * LOG2E, val)
        if sel == 0:
            val = jnp.where(i >= j, val, MASK_VALUE)
        o_ref[0, sel] = val


def _bias_tiles(rel_bias, t):
    return pl.pallas_call(
        functools.partial(_bias_tiles_kernel, t=t),
        grid=(A_HEADS,),
        in_specs=[pl.BlockSpec(memory_space=pltpu.SMEM)],
        out_specs=pl.BlockSpec((1, 2, t, t), lambda h: (h, 0, 0, 0)),
        out_shape=jax.ShapeDtypeStruct((A_HEADS, 2, t, t), F32),
        compiler_params=_params("arbitrary"),
        name="t5_bias_tiles",
    )(rel_bias)


def _attn_kernel(q_ref, k_ref, v_ref, bt_ref, rb_ref, lam_ref, ng_ref, o_ref,
                 kb_ref, vt_ref, qst_ref, sa_ref, sb_ref, m_ref, l_ref, acc_ref, *, t, lam_init):
    h = pl.program_id(1)
    i = pl.program_id(2)

    @pl.when(i == 0)
    def _():
        def fill(c, carry):
            rows = pl.ds(pl.multiple_of(c * t, t), t)
            kb_ref[c] = k_ref[rows, :].astype(BF16)
            vt_ref[c] = v_ref[rows, :].T.astype(BF16)
            return carry

        lax.fori_loop(0, kb_ref.shape[0], fill, 0)

    qt = (q_ref[...] * (A_HEAD_DIM ** -0.5 * LOG2E)).T
    feat = lax.broadcasted_iota(jnp.int32, qt.shape, 0)
    qst_ref[:, 0:t] = jnp.where(feat < A_HEAD_DIM, qt, 0.0).astype(BF16)
    qst_ref[:, t:2 * t] = jnp.where(feat >= A_HEAD_DIM, qt, 0.0).astype(BF16)
    m_ref[...] = jnp.full(m_ref.shape, MASK_VALUE, F32)
    l_ref[...] = jnp.zeros_like(l_ref)
    acc_ref[...] = jnp.zeros_like(acc_ref)

    def scores(kj, s2_ref):
        s2_ref[...] = _dot(kb_ref[kj], qst_ref[...])

    def step(kj, s2_ref, tile_sel, const_bias):
        vt = vt_ref[kj]
        for c in range(2):
            s = s2_ref[:, c * t:(c + 1) * t]
            if tile_sel is not None:
                s = s + bt_ref[0, tile_sel]
            m_old = m_ref[c]
            smax = jnp.max(s, axis=0, keepdims=True)
            if const_bias is not None:
                smax = smax + const_bias
            m_new = jnp.maximum(m_old, smax)
            shift = m_new if const_bias is None else m_new - const_bias
            p = jnp.exp2(s - shift)
            corr = jnp.exp2(m_old - m_new)
            l_ref[c] = corr * l_ref[c] + jnp.sum(p, axis=0, keepdims=True)
            acc_ref[c] = corr * acc_ref[c] + _dot(vt, p.astype(BF16))
            m_ref[c] = m_new

    far_bias = rb_ref[N_BUCKETS - 1, h] * LOG2E

    n_far = i - 1
    odd_far = jnp.logical_and(i >= 2, n_far % 2 == 1)
    scores(0, sa_ref)

    def far_pair(pj, carry):
        scores(2 * pj + 1, sb_ref)
        step(2 * pj, sa_ref, None, far_bias)
        scores(2 * pj + 2, sa_ref)
        step(2 * pj + 1, sb_ref, None, far_bias)
        return carry

    lax.fori_loop(0, jnp.maximum(n_far, 0) // 2, far_pair, 0)

    @pl.when(odd_far)
    def _():
        scores(i - 1, sb_ref)
        step(i - 2, sa_ref, None, far_bias)
        scores(i, sa_ref)
        step(i - 1, sb_ref, 1, None)
        step(i, sa_ref, 0, None)

    @pl.when(jnp.logical_and(i >= 1, jnp.logical_not(odd_far)))
    def _():
        scores(i, sb_ref)
        step(i - 1, sa_ref, 1, None)
        step(i, sb_ref, 0, None)

    @pl.when(i == 0)
    def _():
        step(0, sa_ref, 0, None)

    lmb = lam_ref[...]
    lam = (jnp.exp(jnp.sum(lmb[0:1] * lmb[1:2], axis=1, keepdims=True))
           - jnp.exp(jnp.sum(lmb[2:3] * lmb[3:4], axis=1, keepdims=True)) + lam_init)
    o = (acc_ref[0] / l_ref[0] - lam * (acc_ref[1] / l_ref[1])).T
    o = o * lax.rsqrt(jnp.mean(o * o, axis=-1, keepdims=True) + LN_EPS) * ng_ref[...]
    o_ref[...] = (o * (1.0 - lam_init)).astype(o_ref.dtype)


def _diff_attention(h, bias_tiles, rel_bias, diff_lambda, norm_g, *, bsz, seq, t, layer_idx):
    n = bsz * seq
    nq = seq // t
    lam_init = 0.8 - 0.6 * math.exp(-0.3 * layer_idx)
    kcol = GROUP_WIDTH // LANES
    return pl.pallas_call(
        functools.partial(_attn_kernel, t=t, lam_init=lam_init),
        grid=(bsz, A_HEADS, nq),
        in_specs=[
            pl.BlockSpec((t, LANES), lambda b, hh, i: (b * nq + i, hh)),
            pl.BlockSpec((seq, LANES), lambda b, hh, i: (b, kcol + hh)),
            pl.BlockSpec((seq, LANES), lambda b, hh, i: (b, 2 * kcol + hh)),
            pl.BlockSpec((1, 2, t, t), lambda b, hh, i: (hh, 0, 0, 0)),
            pl.BlockSpec(memory_space=pltpu.SMEM),
            pl.BlockSpec((4, A_HEAD_DIM), lambda b, hh, i: (0, 0)),
            pl.BlockSpec((1, LANES), lambda b, hh, i: (0, 0)),
        ],
        out_specs=pl.BlockSpec((t, LANES), lambda b, hh, i: (b * nq + i, hh)),
        out_shape=jax.ShapeDtypeStruct((n, GROUP_WIDTH), BF16),
        scratch_shapes=[
            pltpu.VMEM((nq, t, LANES), BF16), pltpu.VMEM((nq, LANES, t), BF16),
            pltpu.VMEM((LANES, 2 * t), BF16), pltpu.VMEM((t, 2 * t), F32), pltpu.VMEM((t, 2 * t), F32),
            pltpu.VMEM((2, 1, t), F32), pltpu.VMEM((2, 1, t), F32), pltpu.VMEM((2, LANES, t), F32),
        ],
        compiler_params=_params("parallel", "parallel", "arbitrary"),
        name="diff_attention",
    )(h, h, h, bias_tiles, rel_bias, diff_lambda, norm_g)


def _rope_kernel(pos_ref, inv_ref, cos_ref, sin_ref):
    ang = pos_ref[...].astype(F32) * inv_ref[...]
    lane = lax.broadcasted_iota(jnp.int32, ang.shape, 1)
    cos_ref[...] = jnp.cos(ang)
    sn = jnp.sin(ang)
    sin_ref[...] = jnp.where(lane < B_HEAD_DIM // 2, -sn, sn)


def _rope_tables(pos_b, inv_b, *, tr=1024):
    n = pos_b.shape[0]
    return pl.pallas_call(
        _rope_kernel,
        grid=(n // tr,),
        in_specs=[pl.BlockSpec((tr, LANES), lambda i: (i, 0)), pl.BlockSpec((1, LANES), lambda i: (0, 0))],
        out_specs=[pl.BlockSpec((tr, LANES), lambda i: (i, 0))] * 2,
        out_shape=[jax.ShapeDtypeStruct((n, LANES), F32)] * 2,
        compiler_params=_params("parallel"),
        name="rope_tables",
    )(pos_b, inv_b)


def _retention_consts():
    log_g = np.log(1.0 - 2.0 ** (-5.0 - np.arange(B_HEADS, dtype=np.float64)))
    j = np.arange(B_CHUNK, dtype=np.float64)
    diff = j[:, None] - j[None, :]
    decay_mask = np.where(diff >= 0, np.exp(log_g[:, None, None] * np.maximum(diff, 0.0)), 0.0)
    q_dec = np.exp(log_g[:, None] * (j[None, :] + 1.0))
    k_dec = np.exp(log_g[:, None] * (B_CHUNK - 1.0 - j[None, :]))
    chunk_dec = np.exp(log_g * B_CHUNK)
    bc = lambda a: np.ascontiguousarray(np.broadcast_to(a[:, :, None], (B_HEADS, B_CHUNK, LANES)))
    return (decay_mask.astype(np.float32), bc(q_dec).astype(np.float32), bc(k_dec).astype(np.float32),
            [float(c) for c in chunk_dec])


def _retention_kernel(q_ref, k_ref, v_ref, g_ref, cos_ref, sin_ref, dm_ref, qd_ref, kd_ref, o_ref, st_ref,
                      *, chunk_dec):
    @pl.when(pl.program_id(1) == 0)
    def _():
        st_ref[...] = jnp.zeros_like(st_ref)

    tr = q_ref.shape[0]
    half = B_HEAD_DIM // 2
    for c in range(tr // B_CHUNK):
        rows = slice(c * B_CHUNK, (c + 1) * B_CHUNK)
        cos = cos_ref[rows, :]
        sin = sin_ref[rows, :]
        for hh in range(B_HEADS):
            cols = slice(hh * B_HEAD_DIM, (hh + 1) * B_HEAD_DIM)
            q = q_ref[rows, cols]
            k = k_ref[rows, cols]
            q = q * cos + pltpu.roll(q, half, 1) * sin
            k = (k * cos + pltpu.roll(k, half, 1) * sin) * (B_HEAD_DIM ** -0.5)
            qb = q.astype(BF16)
            vb = v_ref[rows, cols].astype(BF16)
            scores = _dot_nt(qb, k.astype(BF16)) * dm_ref[hh]
            st = st_ref[hh]
            o = _dot(scores.astype(BF16), vb) + _dot(qb, st.astype(BF16)) * qd_ref[hh]
            st_ref[hh] = st * chunk_dec[hh] + _dot((k * kd_ref[hh]).T.astype(BF16), vb)
            mu = jnp.mean(o, axis=-1, keepdims=True)
            d = o - mu
            var = jnp.mean(d * d, axis=-1, keepdims=True)
            gate = g_ref[rows, cols]
            o_ref[rows, cols] = (d * lax.rsqrt(var + LN_EPS) * (gate * _sigmoid(gate))).astype(o_ref.dtype)


def _retention(h, cos_t, sin_t, *, bsz, seq, tr=512):
    n = bsz * seq
    ns = seq // tr
    dm, qd, kd, chunk_dec = _retention_consts()
    row = lambda b, s: b * ns + s
    const3 = pl.BlockSpec((B_HEADS, B_CHUNK, LANES), lambda b, s: (0, 0, 0))
    return pl.pallas_call(
        functools.partial(_retention_kernel, chunk_dec=chunk_dec),
        grid=(bsz, ns),
        in_specs=[
            pl.BlockSpec((tr, GROUP_WIDTH), lambda b, s: (row(b, s), 3)),
            pl.BlockSpec((tr, GROUP_WIDTH), lambda b, s: (row(b, s), 4)),
            pl.BlockSpec((tr, GROUP_WIDTH), lambda b, s: (row(b, s), 5)),
            pl.BlockSpec((tr, GROUP_WIDTH), lambda b, s: (row(b, s), 6)),
            pl.BlockSpec((tr, LANES), lambda b, s: (row(b, s), 0)),
            pl.BlockSpec((tr, LANES), lambda b, s: (row(b, s), 0)),
            const3, const3, const3,
        ],
        out_specs=pl.BlockSpec((tr, GROUP_WIDTH), lambda b, s: (row(b, s), 0)),
        out_shape=jax.ShapeDtypeStruct((n, GROUP_WIDTH), BF16),
        scratch_shapes=[pltpu.VMEM((B_HEADS, B_HEAD_DIM, B_HEAD_DIM), F32)],
        compiler_params=_params("parallel", "arbitrary"),
        name="retention",
    )(h, h, h, h, cos_t, sin_t, jnp.asarray(dm), jnp.asarray(qd), jnp.asarray(kd))


def _gelu(x):
    return 0.5 * x * (1.0 + lax.erf(x * (0.5 ** 0.5)))


def _gmlp_kernel(u_ref, v_ref, g_ref, b_ref, ws_ref, bs_ref, o_ref):
    tr = u_ref.shape[0]
    v = _layer_norm(_gelu(v_ref[...]), g_ref[...], b_ref[...]).astype(BF16)
    ti = lax.broadcasted_iota(jnp.int32, (C_CHUNK, C_CHUNK), 0)
    si = lax.broadcasted_iota(jnp.int32, (C_CHUNK, C_CHUNK), 1)
    for gi in range(C_GROUPS):
        cols = slice(gi * LANES, (gi + 1) * LANES)
        w = jnp.where(ti >= si, ws_ref[gi], 0.0).astype(BF16)
        for c in range(tr // C_CHUNK):
            rows = slice(c * C_CHUNK, (c + 1) * C_CHUNK)
            mixed = _dot(w, v[rows, cols]) + bs_ref[gi]
            o_ref[rows, cols] = (_gelu(u_ref[rows, cols]) * mixed).astype(o_ref.dtype)


def _spatial_gating(h, ln_g, ln_b, w_s, bs_b, *, tr=512):
    n = h.shape[0]
    const3 = pl.BlockSpec((C_GROUPS, C_CHUNK, LANES), lambda i: (0, 0, 0))
    vec = pl.BlockSpec((1, GROUP_WIDTH), lambda i: (0, 0))
    return pl.pallas_call(
        _gmlp_kernel,
        grid=(n // tr,),
        in_specs=[
            pl.BlockSpec((tr, GROUP_WIDTH), lambda i: (i, 7)),
            pl.BlockSpec((tr, GROUP_WIDTH), lambda i: (i, 8)),
            vec, vec, const3, const3,
        ],
        out_specs=pl.BlockSpec((tr, GROUP_WIDTH), lambda i: (i, 0)),
        out_shape=jax.ShapeDtypeStruct((n, GROUP_WIDTH), BF16),
        compiler_params=_params("parallel"),
        name="spatial_gating",
    )(h, h, ln_g, ln_b, w_s, bs_b)


def _hgrn_kernel(q_ref, z_ref, i_ref, g_ref, lbl_ref, ng_ref, o_ref, st_ref, oc_ref, *, layer_idx):
    @pl.when(pl.program_id(1) == 0)
    def _():
        st_ref[...] = jnp.zeros_like(st_ref)

    tr = q_ref.shape[0]
    cs, sub = D_CHUNK, D_SUB

    logits = lbl_ref[...]
    e = jnp.exp(logits - jnp.max(logits, axis=0, keepdims=True))
    soft = e / jnp.sum(e, axis=0, keepdims=True)
    lb = jnp.sum(soft[0:layer_idx + 1], axis=0, keepdims=True) - soft[0:1]
    lb = jnp.maximum(lb, LB_FLOOR)

    ti = lax.broadcasted_iota(jnp.int32, (cs, cs), 0)
    si = lax.broadcasted_iota(jnp.int32, (cs, cs), 1)
    tri = jnp.where(ti >= si, 1.0, 0.0).astype(F32)
    ones = jnp.ones((LANES, LANES), BF16)
    row_in_sub = lax.broadcasted_iota(jnp.int32, (cs, LANES), 0) % sub

    for c in range(tr // cs):
        rows = slice(c * cs, (c + 1) * cs)
        z = z_ref[rows, :]
        ez = jnp.exp(-jnp.abs(z))
        r = 1.0 / (1.0 + ez)
        sig_pos = jnp.where(z >= 0, r, ez * r)
        sig_neg = jnp.where(z >= 0, ez * r, r)
        log_f = jnp.log(sig_pos + lb * sig_neg)
        key_all = (1.0 - lb) * sig_neg
        b_all = jnp.dot(tri, log_f, preferred_element_type=F32, precision=lax.Precision.HIGHEST)
        for hh in range(D_HEADS):
            cols = slice(hh * LANES, (hh + 1) * LANES)
            q = q_ref[rows, cols]
            k = key_all[:, cols]
            v = i_ref[rows, cols]
            b = b_all[:, cols]
            vb = v.astype(BF16)
            st = st_ref[hh]
            o = _dot_nt((q * jnp.exp(b)).astype(BF16), st.astype(BF16))
            a0 = _dot((q * k).astype(BF16), ones)
            o = o + a0 * v
            for d in range(1, sub):
                valid = row_in_sub >= d
                kd = pltpu.roll(k, d, 0)
                bd = pltpu.roll(b, d, 0)
                vd = pltpu.roll(v, d, 0)
                p = q * kd * jnp.exp(jnp.where(valid, b - bd, MASK_VALUE))
                o = o + _dot(p.astype(BF16), ones) * vd
            slabs = [o[0:sub]]
            for si_ in range(1, cs // sub):
                lo = si_ * sub
                bref = b[lo - 1:lo, :]
                qt = (q[lo:lo + sub] * jnp.exp(b[lo:lo + sub] - bref)).astype(BF16)
                kt = (k[0:lo] * jnp.exp(bref - b[0:lo])).astype(BF16)
                a = _dot_nt(qt, kt)
                slabs.append(o[lo:lo + sub] + _dot(a.astype(BF16), vb[0:lo]))
            oc_ref[rows, cols] = jnp.concatenate(slabs, axis=0)
            b_last = b[cs - 1:cs, :]
            kdec = (k * jnp.exp(b_last - b)).astype(BF16)
            st_ref[hh] = st * jnp.exp(b_last) + _dot(v.T.astype(BF16), kdec)

    o = oc_ref[...]
    o = o * lax.rsqrt(jnp.mean(o * o, axis=-1, keepdims=True) + LN_EPS) * ng_ref[...]
    gate = g_ref[...]
    o_ref[...] = (o * (gate * _sigmoid(gate))).astype(o_ref.dtype)


def _hgrn2(h, lb_logits, norm_g, *, bsz, seq, layer_idx, tr=256):
    n = bsz * seq
    ns = seq // tr
    row = lambda b, s: b * ns + s
    return pl.pallas_call(
        functools.partial(_hgrn_kernel, layer_idx=layer_idx),
        grid=(bsz, ns),
        in_specs=[
            pl.BlockSpec((tr, GROUP_WIDTH), lambda b, s: (row(b, s), 9)),
            pl.BlockSpec((tr, GROUP_WIDTH), lambda b, s: (row(b, s), 10)),
            pl.BlockSpec((tr, GROUP_WIDTH), lambda b, s: (row(b, s), 11)),
            pl.BlockSpec((tr, GROUP_WIDTH), lambda b, s: (row(b, s), 12)),
            pl.BlockSpec((DEPTH, GROUP_WIDTH), lambda b, s: (0, 0)),
            pl.BlockSpec((1, GROUP_WIDTH), lambda b, s: (0, 0)),
        ],
        out_specs=pl.BlockSpec((tr, GROUP_WIDTH), lambda b, s: (row(b, s), 0)),
        out_shape=jax.ShapeDtypeStruct((n, GROUP_WIDTH), BF16),
        scratch_shapes=[pltpu.VMEM((D_HEADS, LANES, LANES), F32), pltpu.VMEM((tr, GROUP_WIDTH), F32)],
        compiler_params=_params("parallel", "arbitrary"),
        name="hgrn2",
    )(h, h, h, h, lb_logits, norm_g)


def _out_proj_kernel(x_ref, a_ref, b_ref, c_ref, d_ref, w_ref, g_ref, bb_ref, o_ref):
    acc = _dot(a_ref[...], w_ref[0])
    acc += _dot(b_ref[...], w_ref[1])
    acc += _dot(c_ref[...], w_ref[2])
    acc += _dot(d_ref[...], w_ref[3])
    o_ref[...] = _layer_norm(ALPHA * x_ref[...] + acc, g_ref[...], bb_ref[...])


def _out_proj_ln(x, parts, w, g, b, *, tm=512):
    n, d = x.shape
    part = pl.BlockSpec((tm, GROUP_WIDTH), lambda i: (i, 0))
    vec = pl.BlockSpec((1, d), lambda i: (0, 0))
    return pl.pallas_call(
        _out_proj_kernel,
        grid=(n // tm,),
        in_specs=[pl.BlockSpec((tm, d), lambda i: (i, 0)), part, part, part, part,
                  pl.BlockSpec((4, GROUP_WIDTH, d), lambda i: (0, 0, 0)), vec, vec],
        out_specs=pl.BlockSpec((tm, d), lambda i: (i, 0)),
        out_shape=jax.ShapeDtypeStruct((n, d), F32),
        compiler_params=_params("parallel"),
        name="out_proj_ln",
    )(x, *parts, w, g, b)


def _ple_kernel(x_ref, p_ref, wg_ref, we_ref, g_ref, b_ref, o_ref):
    x = x_ref[...]
    gate = _sigmoid(_dot(x.astype(BF16), wg_ref[...]))
    emb = _dot(p_ref[...].astype(BF16), we_ref[...])
    o_ref[...] = _layer_norm(ALPHA * x + gate * emb, g_ref[...], b_ref[...])


def _ple_ln(x, p, wg, we, g, b, *, tm=512):
    n, d = x.shape
    vec = pl.BlockSpec((1, d), lambda i: (0, 0))
    return pl.pallas_call(
        _ple_kernel,
        grid=(n // tm,),
        in_specs=[pl.BlockSpec((tm, d), lambda i: (i, 0)), pl.BlockSpec((tm, PLE_DIM), lambda i: (i, 0)),
                  pl.BlockSpec((d, d), lambda i: (0, 0)), pl.BlockSpec((PLE_DIM, d), lambda i: (0, 0)), vec, vec],
        out_specs=pl.BlockSpec((tm, d), lambda i: (i, 0)),
        out_shape=jax.ShapeDtypeStruct((n, d), F32),
        compiler_params=_params("parallel"),
        name="ple_ln",
    )(x, p, wg, we, g, b)


def _rotary_column_order():
    per_head = np.concatenate([np.arange(0, B_HEAD_DIM, 2), np.arange(1, B_HEAD_DIM, 2)])
    return np.concatenate([hh * B_HEAD_DIM + per_head for hh in range(B_HEADS)])


def _mix_in_weight(w):
    perm = _rotary_column_order()
    cols = np.arange(IN_COLS)
    for blk in (3, 4):
        cols[blk * GROUP_WIDTH:(blk + 1) * GROUP_WIDTH] = blk * GROUP_WIDTH + perm
    return w[:, cols].astype(BF16)


def kernel(x, p, positions, ffn1_w_in, ffn1_w_out, w_mix_in, w_mix_out, rel_bias, diff_lambda, diff_norm_g,
           gmlp_ln_g, gmlp_ln_b, gmlp_w_s, gmlp_b_s, hgrn_lb_logits, hgrn_norm_g, ffn2_w_in, ffn2_w_out,
           ple_w_gate, ple_w_proj, ln_g, ln_b):
    bsz, seq, d = x.shape
    n = bsz * seq
    t_attn = min(512, seq)
    xs = x.reshape(n, d)

    pos_b = jnp.broadcast_to(positions.reshape(n, 1), (n, LANES))
    inv = ROPE_BASE ** (-jnp.linspace(0.0, 1.0, B_HEAD_DIM // 2, dtype=F32))
    inv_b = jnp.concatenate([inv, inv]).reshape(1, LANES)
    cos_t, sin_t = _rope_tables(pos_b, inv_b, tr=min(1024, n))
    bias_tiles = _bias_tiles(rel_bias, t_attn)

    for i in range(DEPTH):
        lng = ln_g[i].reshape(4, 1, d)
        lnb = ln_b[i].reshape(4, 1, d)
        xs = _ffn_ln(xs, ffn1_w_in[i].astype(BF16), ffn1_w_out[i].astype(BF16), lng[0], lnb[0])
        h = _in_proj(xs, _mix_in_weight(w_mix_in[i]))
        out_a = _diff_attention(h, bias_tiles, rel_bias, diff_lambda[i], diff_norm_g[i].reshape(1, LANES),
                                bsz=bsz, seq=seq, t=t_attn, layer_idx=i)
        out_b = _retention(h, cos_t, sin_t, bsz=bsz, seq=seq)
        bs_b = jnp.broadcast_to(gmlp_b_s[i][:, :, None], (C_GROUPS, C_CHUNK, LANES))
        out_c = _spatial_gating(h, gmlp_ln_g[i].reshape(1, GROUP_WIDTH), gmlp_ln_b[i].reshape(1, GROUP_WIDTH),
                                gmlp_w_s[i], bs_b)
        out_d = _hgrn2(h, hgrn_lb_logits, hgrn_norm_g[i].reshape(1, GROUP_WIDTH), bsz=bsz, seq=seq, layer_idx=i)
        xs = _out_proj_ln(xs, (out_a, out_b, out_c, out_d),
                          w_mix_out[i].astype(BF16).reshape(4, GROUP_WIDTH, d), lng[1], lnb[1])
        xs = _ffn_ln(xs, ffn2_w_in[i].astype(BF16), ffn2_w_out[i].astype(BF16), lng[2], lnb[2])
        xs = _ple_ln(xs, p[i].reshape(n, PLE_DIM), ple_w_gate[i].astype(BF16), ple_w_proj[i].astype(BF16),
                     lng[3], lnb[3])
    return xs.reshape(bsz, seq, d)
```

```python
import functools
import math

import numpy as np
import jax
import jax.numpy as jnp
from jax import lax
from jax.experimental import pallas as pl
from jax.experimental.pallas import tpu as pltpu

D_MODEL = 2048
DEPTH = 2
PLE_DIM = 256
D_FF = 5632
GROUP_WIDTH = 512

A_HEADS = 4
A_HEAD_DIM = 64
N_BUCKETS = 32
MAX_DISTANCE = 128

B_HEADS = 4
B_HEAD_DIM = 128
B_CHUNK = 128
ROPE_BASE = 10000.0

C_GROUPS = 4
C_CHUNK = 128

D_HEADS = 4
D_CHUNK = 64
D_LEVELS = 6

IN_COLS = 13 * GROUP_WIDTH
ALPHA = (2 * DEPTH) ** 0.25
LN_EPS = 1e-5
MASK_VALUE = -1e30
LB_FLOOR = 1e-30
LOG2E = math.log2(math.e)

LANES = 128
VMEM_LIMIT = 56 * 1024 * 1024

F32 = jnp.float32
BF16 = jnp.bfloat16


def _dot(a, b):
    return jnp.dot(a, b, preferred_element_type=F32)


def _dot_nt(a, b):
    return lax.dot_general(a, b, (((1,), (1,)), ((), ())), preferred_element_type=F32)


def _sigmoid(x):
    return 1.0 / (1.0 + jnp.exp(-x))


def _layer_norm(y, g, b):
    mu = jnp.mean(y, axis=-1, keepdims=True)
    d = y - mu
    var = jnp.mean(d * d, axis=-1, keepdims=True)
    return d * lax.rsqrt(var + LN_EPS) * g + b


def _params(*sem):
    return pltpu.CompilerParams(dimension_semantics=sem, vmem_limit_bytes=VMEM_LIMIT)


def _ffn_kernel(x_ref, wg_ref, wu_ref, wo_ref, g_ref, b_ref, o_ref, xb_ref):
    j = pl.program_id(1)

    @pl.when(j == 0)
    def _():
        xb_ref[...] = x_ref[...].astype(BF16)
        o_ref[...] = jnp.zeros_like(o_ref)

    xb = xb_ref[...]
    hg = _dot(xb, wg_ref[...])
    hu = _dot(xb, wu_ref[...])
    a = (hg * _sigmoid(hg) * hu).astype(BF16)
    o_ref[...] += _dot(a, wo_ref[...])

    @pl.when(j == pl.num_programs(1) - 1)
    def _():
        y = ALPHA * x_ref[...] + 0.5 * o_ref[...]
        o_ref[...] = _layer_norm(y, g_ref[...], b_ref[...])


def _ffn_ln(x, w_in, w_out, g, b, *, tm=512, tf=512):
    n, d = x.shape
    f = w_out.shape[0]
    nf = f // tf
    return pl.pallas_call(
        _ffn_kernel,
        grid=(n // tm, nf),
        in_specs=[
            pl.BlockSpec((tm, d), lambda i, j: (i, 0)),
            pl.BlockSpec((d, tf), lambda i, j: (0, j)),
            pl.BlockSpec((d, tf), lambda i, j: (0, j + nf)),
            pl.BlockSpec((tf, d), lambda i, j: (j, 0)),
            pl.BlockSpec((1, d), lambda i, j: (0, 0)),
            pl.BlockSpec((1, d), lambda i, j: (0, 0)),
        ],
        out_specs=pl.BlockSpec((tm, d), lambda i, j: (i, 0)),
        out_shape=jax.ShapeDtypeStruct((n, d), F32),
        scratch_shapes=[pltpu.VMEM((tm, d), BF16)],
        compiler_params=_params("parallel", "arbitrary"),
        name="ffn_ln",
    )(x, w_in, w_in, w_out, g, b)


def _in_proj_kernel(x_ref, w_ref, o_ref, xb_ref):
    @pl.when(pl.program_id(1) == 0)
    def _():
        xb_ref[...] = x_ref[...].astype(BF16)

    o_ref[...] = _dot(xb_ref[...], w_ref[...])


def _in_proj(x, w, *, tm=1024, tn=512):
    n, d = x.shape
    c = w.shape[1]
    return pl.pallas_call(
        _in_proj_kernel,
        grid=(n // tm, c // tn),
        in_specs=[
            pl.BlockSpec((tm, d), lambda i, j: (i, 0)),
            pl.BlockSpec((d, tn), lambda i, j: (0, j)),
        ],
        out_specs=pl.BlockSpec((tm, tn), lambda i, j: (i, j)),
        out_shape=jax.ShapeDtypeStruct((n, c), F32),
        scratch_shapes=[pltpu.VMEM((tm, d), BF16)],
        compiler_params=_params("parallel", "arbitrary"),
        name="in_proj",
    )(x, w)


def _bias_tiles_kernel(rb_ref, o_ref, *, t):
    h = pl.program_id(0)
    j = lax.broadcasted_iota(jnp.int32, (t, t), 0)
    i = lax.broadcasted_iota(jnp.int32, (t, t), 1)
    max_exact = N_BUCKETS // 2
    for sel in range(2):
        n = jnp.maximum(i - j + sel * t, 0)
        nf = jnp.maximum(n, 1).astype(F32)
        large = max_exact + (jnp.log(nf / max_exact) / math.log(MAX_DISTANCE / max_exact)
                             * (N_BUCKETS - max_exact)).astype(jnp.int32)
        large = jnp.minimum(large, N_BUCKETS - 1)
        bucket = jnp.where(n < max_exact, n, large)
        val = jnp.zeros((t, t), F32)
        for bkt in range(N_BUCKETS):
            val = jnp.where(bucket == bkt, rb_ref[bkt, h] * LOG2E, val)
        if sel == 0:
            val = jnp.where(i >= j, val, MASK_VALUE)
        o_ref[0, sel] = val


def _bias_tiles(rel_bias, t):
    return pl.pallas_call(
        functools.partial(_bias_tiles_kernel, t=t),
        grid=(A_HEADS,),
        in_specs=[pl.BlockSpec(memory_space=pltpu.SMEM)],
        out_specs=pl.BlockSpec((1, 2, t, t), lambda h: (h, 0, 0, 0)),
        out_shape=jax.ShapeDtypeStruct((A_HEADS, 2, t, t), F32),
        compiler_params=_params("arbitrary"),
        name="t5_bias_tiles",
    )(rel_bias)


def _attn_kernel(q_ref, k_ref, v_ref, bt_ref, rb_ref, lam_ref, ng_ref, o_ref,
                 kb_ref, vt_ref, qst_ref, sa_ref, sb_ref, m_ref, l_ref, acc_ref, *, t, lam_init):
    h = pl.program_id(1)
    i = pl.program_id(2)

    @pl.when(i == 0)
    def _():
        def fill(c, carry):
            rows = pl.ds(pl.multiple_of(c * t, t), t)
            kb_ref[c] = k_ref[rows, :].astype(BF16)
            vt_ref[c] = v_ref[rows, :].T.astype(BF16)
            return carry

        lax.fori_loop(0, kb_ref.shape[0], fill, 0)

    qt = (q_ref[...] * (A_HEAD_DIM ** -0.5 * LOG2E)).T
    feat = lax.broadcasted_iota(jnp.int32, qt.shape, 0)
    qst_ref[:, 0:t] = jnp.where(feat < A_HEAD_DIM, qt, 0.0).astype(BF16)
    qst_ref[:, t:2 * t] = jnp.where(feat >= A_HEAD_DIM, qt, 0.0).astype(BF16)
    m_ref[...] = jnp.full(m_ref.shape, MASK_VALUE, F32)
    l_ref[...] = jnp.zeros_like(l_ref)
    acc_ref[...] = jnp.zeros_like(acc_ref)

    def scores(kj, s2_ref):
        s2_ref[...] = _dot(kb_ref[kj], qst_ref[...])

    def step(kj, s2_ref, tile_sel, const_bias):
        vt = vt_ref[kj]
        for c in range(2):
            s = s2_ref[:, c * t:(c + 1) * t]
            if tile_sel is not None:
                s = s + bt_ref[0, tile_sel]
            m_old = m_ref[c]
            smax = jnp.max(s, axis=0, keepdims=True)
            if const_bias is not None:
                smax = smax + const_bias
            m_new = jnp.maximum(m_old, smax)
            shift = m_new if const_bias is None else m_new - const_bias
            p = jnp.exp2(s - shift)
            corr = jnp.exp2(m_old - m_new)
            l_ref[c] = corr * l_ref[c] + jnp.sum(p, axis=0, keepdims=True)
            acc_ref[c] = corr * acc_ref[c] + _dot(vt, p.astype(BF16))
            m_ref[c] = m_new

    far_bias = rb_ref[N_BUCKETS - 1, h] * LOG2E

    n_far = i - 1
    odd_far = jnp.logical_and(i >= 2, n_far % 2 == 1)
    scores(0, sa_ref)

    def far_pair(pj, carry):
        scores(2 * pj + 1, sb_ref)
        step(2 * pj, sa_ref, None, far_bias)
        scores(2 * pj + 2, sa_ref)
        step(2 * pj + 1, sb_ref, None, far_bias)
        return carry

    lax.fori_loop(0, jnp.maximum(n_far, 0) // 2, far_pair, 0)

    @pl.when(odd_far)
    def _():
        scores(i - 1, sb_ref)
        step(i - 2, sa_ref, None, far_bias)
        scores(i, sa_ref)
        step(i - 1, sb_ref, 1, None)
        step(i, sa_ref, 0, None)

    @pl.when(jnp.logical_and(i >= 1, jnp.logical_not(odd_far)))
    def _():
        scores(i, sb_ref)
        step(i - 1, sa_ref, 1, None)
        step(i, sb_ref, 0, None)

    @pl.when(i == 0)
    def _():
        step(0, sa_ref, 0, None)

    lmb = lam_ref[...]
    lam = (jnp.exp(jnp.sum(lmb[0:1] * lmb[1:2], axis=1, keepdims=True))
           - jnp.exp(jnp.sum(lmb[2:3] * lmb[3:4], axis=1, keepdims=True)) + lam_init)
    o = (acc_ref[0] / l_ref[0] - lam * (acc_ref[1] / l_ref[1])).T
    o = o * lax.rsqrt(jnp.mean(o * o, axis=-1, keepdims=True) + LN_EPS) * ng_ref[...]
    o_ref[...] = (o * (1.0 - lam_init)).astype(o_ref.dtype)


def _diff_attention(h, bias_tiles, rel_bias, diff_lambda, norm_g, *, bsz, seq, t, layer_idx):
    n = bsz * seq
    nq = seq // t
    lam_init = 0.8 - 0.6 * math.exp(-0.3 * layer_idx)
    kcol = GROUP_WIDTH // LANES
    return pl.pallas_call(
        functools.partial(_attn_kernel, t=t, lam_init=lam_init),
        grid=(bsz, A_HEADS, nq),
        in_specs=[
            pl.BlockSpec((t, LANES), lambda b, hh, i: (b * nq + i, hh)),
            pl.BlockSpec((seq, LANES), lambda b, hh, i: (b, kcol + hh)),
            pl.BlockSpec((seq, LANES), lambda b, hh, i: (b, 2 * kcol + hh)),
            pl.BlockSpec((1, 2, t, t), lambda b, hh, i: (hh, 0, 0, 0)),
            pl.BlockSpec(memory_space=pltpu.SMEM),
            pl.BlockSpec((4, A_HEAD_DIM), lambda b, hh, i: (0, 0)),
            pl.BlockSpec((1, LANES), lambda b, hh, i: (0, 0)),
        ],
        out_specs=pl.BlockSpec((t, LANES), lambda b, hh, i: (b * nq + i, hh)),
        out_shape=jax.ShapeDtypeStruct((n, GROUP_WIDTH), BF16),
        scratch_shapes=[
            pltpu.VMEM((nq, t, LANES), BF16), pltpu.VMEM((nq, LANES, t), BF16),
            pltpu.VMEM((LANES, 2 * t), BF16), pltpu.VMEM((t, 2 * t), F32), pltpu.VMEM((t, 2 * t), F32),
            pltpu.VMEM((2, 1, t), F32), pltpu.VMEM((2, 1, t), F32), pltpu.VMEM((2, LANES, t), F32),
        ],
        compiler_params=_params("parallel", "parallel", "arbitrary"),
        name="diff_attention",
    )(h, h, h, bias_tiles, rel_bias, diff_lambda, norm_g)


def _rope_kernel(pos_ref, inv_ref, cos_ref, sin_ref):
    ang = pos_ref[...].astype(F32) * inv_ref[...]
    lane = lax.broadcasted_iota(jnp.int32, ang.shape, 1)
    cos_ref[...] = jnp.cos(ang)
    sn = jnp.sin(ang)
    sin_ref[...] = jnp.where(lane < B_HEAD_DIM // 2, -sn, sn)


def _rope_tables(pos_b, inv_b, *, tr=1024):
    n = pos_b.shape[0]
    return pl.pallas_call(
        _rope_kernel,
        grid=(n // tr,),
        in_specs=[pl.BlockSpec((tr, LANES), lambda i: (i, 0)), pl.BlockSpec((1, LANES), lambda i: (0, 0))],
        out_specs=[pl.BlockSpec((tr, LANES), lambda i: (i, 0))] * 2,
        out_shape=[jax.ShapeDtypeStruct((n, LANES), F32)] * 2,
        compiler_params=_params("parallel"),
        name="rope_tables",
    )(pos_b, inv_b)


def _retention_consts():
    log_g = np.log(1.0 - 2.0 ** (-5.0 - np.arange(B_HEADS, dtype=np.float64)))
    j = np.arange(B_CHUNK, dtype=np.float64)
    diff = j[:, None] - j[None, :]
    decay_mask = np.where(diff >= 0, np.exp(log_g[:, None, None] * np.maximum(diff, 0.0)), 0.0)
    q_dec = np.exp(log_g[:, None] * (j[None, :] + 1.0))
    k_dec = np.exp(log_g[:, None] * (B_CHUNK - 1.0 - j[None, :]))
    chunk_dec = np.exp(log_g * B_CHUNK)
    bc = lambda a: np.ascontiguousarray(np.broadcast_to(a[:, :, None], (B_HEADS, B_CHUNK, LANES)))
    return (decay_mask.astype(np.float32), bc(q_dec).astype(np.float32), bc(k_dec).astype(np.float32),
            [float(c) for c in chunk_dec])


def _retention_kernel(q_ref, k_ref, v_ref, g_ref, cos_ref, sin_ref, dm_ref, qd_ref, kd_ref, o_ref, st_ref,
                      *, chunk_dec):
    @pl.when(pl.program_id(1) == 0)
    def _():
        st_ref[...] = jnp.zeros_like(st_ref)

    tr = q_ref.shape[0]
    half = B_HEAD_DIM // 2
    for c in range(tr // B_CHUNK):
        rows = slice(c * B_CHUNK, (c + 1) * B_CHUNK)
        cos = cos_ref[rows, :]
        sin = sin_ref[rows, :]
        for hh in range(B_HEADS):
            cols = slice(hh * B_HEAD_DIM, (hh + 1) * B_HEAD_DIM)
            q = q_ref[rows, cols]
            k = k_ref[rows, cols]
            q = q * cos + pltpu.roll(q, half, 1) * sin
            k = (k * cos + pltpu.roll(k, half, 1) * sin) * (B_HEAD_DIM ** -0.5)
            qb = q.astype(BF16)
            vb = v_ref[rows, cols].astype(BF16)
            scores = _dot_nt(qb, k.astype(BF16)) * dm_ref[hh]
            st = st_ref[hh]
            o = _dot(scores.astype(BF16), vb) + _dot(qb, st.astype(BF16)) * qd_ref[hh]
            st_ref[hh] = st * chunk_dec[hh] + _dot((k * kd_ref[hh]).T.astype(BF16), vb)
            mu = jnp.mean(o, axis=-1, keepdims=True)
            d = o - mu
            var = jnp.mean(d * d, axis=-1, keepdims=True)
            gate = g_ref[rows, cols]
            o_ref[rows, cols] = (d * lax.rsqrt(var + LN_EPS) * (gate * _sigmoid(gate))).astype(o_ref.dtype)


def _retention(h, cos_t, sin_t, *, bsz, seq, tr=512):
    n = bsz * seq
    ns = seq // tr
    dm, qd, kd, chunk_dec = _retention_consts()
    row = lambda b, s: b * ns + s
    const3 = pl.BlockSpec((B_HEADS, B_CHUNK, LANES), lambda b, s: (0, 0, 0))
    return pl.pallas_call(
        functools.partial(_retention_kernel, chunk_dec=chunk_dec),
        grid=(bsz, ns),
        in_specs=[
            pl.BlockSpec((tr, GROUP_WIDTH), lambda b, s: (row(b, s), 3)),
            pl.BlockSpec((tr, GROUP_WIDTH), lambda b, s: (row(b, s), 4)),
            pl.BlockSpec((tr, GROUP_WIDTH), lambda b, s: (row(b, s), 5)),
            pl.BlockSpec((tr, GROUP_WIDTH), lambda b, s: (row(b, s), 6)),
            pl.BlockSpec((tr, LANES), lambda b, s: (row(b, s), 0)),
            pl.BlockSpec((tr, LANES), lambda b, s: (row(b, s), 0)),
            const3, const3, const3,
        ],
        out_specs=pl.BlockSpec((tr, GROUP_WIDTH), lambda b, s: (row(b, s), 0)),
        out_shape=jax.ShapeDtypeStruct((n, GROUP_WIDTH), BF16),
        scratch_shapes=[pltpu.VMEM((B_HEADS, B_HEAD_DIM, B_HEAD_DIM), F32)],
        compiler_params=_params("parallel", "arbitrary"),
        name="retention",
    )(h, h, h, h, cos_t, sin_t, jnp.asarray(dm), jnp.asarray(qd), jnp.asarray(kd))


def _gelu(x):
    return 0.5 * x * (1.0 + lax.erf(x * (0.5 ** 0.5)))


def _gmlp_kernel(u_ref, v_ref, g_ref, b_ref, ws_ref, bs_ref, o_ref):
    tr = u_ref.shape[0]
    v = _layer_norm(_gelu(v_ref[...]), g_ref[...], b_ref[...]).astype(BF16)
    ti = lax.broadcasted_iota(jnp.int32, (C_CHUNK, C_CHUNK), 0)
    si = lax.broadcasted_iota(jnp.int32, (C_CHUNK, C_CHUNK), 1)
    for gi in range(C_GROUPS):
        cols = slice(gi * LANES, (gi + 1) * LANES)
        w = jnp.where(ti >= si, ws_ref[gi], 0.0).astype(BF16)
        for c in range(tr // C_CHUNK):
            rows = slice(c * C_CHUNK, (c + 1) * C_CHUNK)
            mixed = _dot(w, v[rows, cols]) + bs_ref[gi]
            o_ref[rows, cols] = (_gelu(u_ref[rows, cols]) * mixed).astype(o_ref.dtype)


def _spatial_gating(h, ln_g, ln_b, w_s, bs_b, *, tr=512):
    n = h.shape[0]
    const3 = pl.BlockSpec((C_GROUPS, C_CHUNK, LANES), lambda i: (0, 0, 0))
    vec = pl.BlockSpec((1, GROUP_WIDTH), lambda i: (0, 0))
    return pl.pallas_call(
        _gmlp_kernel,
        grid=(n // tr,),
        in_specs=[
            pl.BlockSpec((tr, GROUP_WIDTH), lambda i: (i, 7)),
            pl.BlockSpec((tr, GROUP_WIDTH), lambda i: (i, 8)),
            vec, vec, const3, const3,
        ],
        out_specs=pl.BlockSpec((tr, GROUP_WIDTH), lambda i: (i, 0)),
        out_shape=jax.ShapeDtypeStruct((n, GROUP_WIDTH), BF16),
        compiler_params=_params("parallel"),
        name="spatial_gating",
    )(h, h, ln_g, ln_b, w_s, bs_b)


def _hgrn_consts():
    c = D_CHUNK
    t = np.arange(c)[:, None]
    r = np.arange(c)[None, :]
    ranges = [r <= t, r > t]
    masks = [r == t]
    m = c // 2
    while m >= 1:
        ref = (t // (2 * m)) * 2 * m + m - 1
        second = (t % (2 * m)) >= m
        ranges.append(np.where(second, (r > ref) & (r <= t), (r > t) & (r <= ref)))
        masks.append(((t // (2 * m)) == (r // (2 * m))) & second & ((r % (2 * m)) < m))
        m //= 2
    return (np.concatenate(ranges, axis=0).astype(np.float32), np.stack(masks).astype(np.float32))


def _hgrn_kernel(q_ref, z_ref, i_ref, g_ref, lbl_ref, ng_ref, rs_ref, pm_ref, o_ref, st_ref, oc_ref, *, layer_idx):
    @pl.when(pl.program_id(1) == 0)
    def _():
        st_ref[...] = jnp.zeros_like(st_ref)

    tr = q_ref.shape[0]
    cs = D_CHUNK

    logits = lbl_ref[...]
    e = jnp.exp(logits - jnp.max(logits, axis=0, keepdims=True))
    soft = e / jnp.sum(e, axis=0, keepdims=True)
    lb = jnp.sum(soft[0:layer_idx + 1], axis=0, keepdims=True) - soft[0:1]
    lb = jnp.maximum(lb, LB_FLOOR)

    for c in range(tr // cs):
        rows = slice(c * cs, (c + 1) * cs)
        z = z_ref[rows, :]
        ez = jnp.exp(-jnp.abs(z))
        r = 1.0 / (1.0 + ez)
        sig_pos = jnp.where(z >= 0, r, ez * r)
        sig_neg = jnp.where(z >= 0, ez * r, r)
        log2_f = jnp.log(sig_pos + lb * sig_neg) * LOG2E
        key_all = (1.0 - lb) * sig_neg
        hi = log2_f.astype(BF16)
        lo = (log2_f - hi.astype(F32)).astype(BF16)
        e2 = _dot(rs_ref[...], jnp.concatenate([hi, lo], axis=1))
        e_all = e2[:, 0:GROUP_WIDTH] + e2[:, GROUP_WIDTH:2 * GROUP_WIDTH]
        for hh in range(D_HEADS):
            cols = slice(hh * LANES, (hh + 1) * LANES)
            q = q_ref[rows, cols]
            k = key_all[:, cols]
            vb = i_ref[rows, cols].astype(BF16)
            b = e_all[0:cs, cols]
            rem = e_all[cs:2 * cs, cols]
            st = st_ref[hh]
            o = _dot_nt((q * jnp.exp2(b)).astype(BF16), st.astype(BF16))
            a = jnp.where(pm_ref[0] > 0, _dot_nt(q.astype(BF16), k.astype(BF16)), 0.0)
            for lv in range(D_LEVELS):
                w = jnp.exp2(e_all[(2 + lv) * cs:(3 + lv) * cs, cols])
                a = a + jnp.where(pm_ref[1 + lv] > 0, _dot_nt((q * w).astype(BF16), (k * w).astype(BF16)), 0.0)
            oc_ref[rows, cols] = o + _dot(a.astype(BF16), vb)
            kdec = (k * jnp.exp2(rem)).astype(BF16)
            st_ref[hh] = st * jnp.exp2(b[cs - 1:cs, :]) + _dot(i_ref[rows, cols].T.astype(BF16), kdec)

    o = oc_ref[...]
    o = o * lax.rsqrt(jnp.mean(o * o, axis=-1, keepdims=True) + LN_EPS) * ng_ref[...]
    gate = g_ref[...]
    o_ref[...] = (o * (gate * _sigmoid(gate))).astype(o_ref.dtype)


def _hgrn2(h, lb_logits, norm_g, *, bsz, seq, layer_idx, tr=256):
    n = bsz * seq
    ns = seq // tr
    row = lambda b, s: b * ns + s
    range_sums, pair_masks = _hgrn_consts()
    return pl.pallas_call(
        functools.partial(_hgrn_kernel, layer_idx=layer_idx),
        grid=(bsz, ns),
        in_specs=[
            pl.BlockSpec((tr, GROUP_WIDTH), lambda b, s: (row(b, s), 9)),
            pl.BlockSpec((tr, GROUP_WIDTH), lambda b, s: (row(b, s), 10)),
            pl.BlockSpec((tr, GROUP_WIDTH), lambda b, s: (row(b, s), 11)),
            pl.BlockSpec((tr, GROUP_WIDTH), lambda b, s: (row(b, s), 12)),
            pl.BlockSpec((DEPTH, GROUP_WIDTH), lambda b, s: (0, 0)),
            pl.BlockSpec((1, GROUP_WIDTH), lambda b, s: (0, 0)),
            pl.BlockSpec(range_sums.shape, lambda b, s: (0, 0)),
            pl.BlockSpec(pair_masks.shape, lambda b, s: (0, 0, 0)),
        ],
        out_specs=pl.BlockSpec((tr, GROUP_WIDTH), lambda b, s: (row(b, s), 0)),
        out_shape=jax.ShapeDtypeStruct((n, GROUP_WIDTH), BF16),
        scratch_shapes=[pltpu.VMEM((D_HEADS, LANES, LANES), F32), pltpu.VMEM((tr, GROUP_WIDTH), F32)],
        compiler_params=_params("parallel", "arbitrary"),
        name="hgrn2",
    )(h, h, h, h, lb_logits, norm_g, jnp.asarray(range_sums, BF16), jnp.asarray(pair_masks))


def _out_proj_kernel(x_ref, a_ref, b_ref, c_ref, d_ref, w_ref, g_ref, bb_ref, o_ref):
    acc = _dot(a_ref[...], w_ref[0])
    acc += _dot(b_ref[...], w_ref[1])
    acc += _dot(c_ref[...], w_ref[2])
    acc += _dot(d_ref[...], w_ref[3])
    o_ref[...] = _layer_norm(ALPHA * x_ref[...] + acc, g_ref[...], bb_ref[...])


def _out_proj_ln(x, parts, w, g, b, *, tm=512):
    n, d = x.shape
    part = pl.BlockSpec((tm, GROUP_WIDTH), lambda i: (i, 0))
    vec = pl.BlockSpec((1, d), lambda i: (0, 0))
    return pl.pallas_call(
        _out_proj_kernel,
        grid=(n // tm,),
        in_specs=[pl.BlockSpec((tm, d), lambda i: (i, 0)), part, part, part, part,
                  pl.BlockSpec((4, GROUP_WIDTH, d), lambda i: (0, 0, 0)), vec, vec],
        out_specs=pl.BlockSpec((tm, d), lambda i: (i, 0)),
        out_shape=jax.ShapeDtypeStruct((n, d), F32),
        compiler_params=_params("parallel"),
        name="out_proj_ln",
    )(x, *parts, w, g, b)


def _ple_kernel(x_ref, p_ref, wg_ref, we_ref, g_ref, b_ref, o_ref):
    x = x_ref[...]
    gate = _sigmoid(_dot(x.astype(BF16), wg_ref[...]))
    emb = _dot(p_ref[...].astype(BF16), we_ref[...])
    o_ref[...] = _layer_norm(ALPHA * x + gate * emb, g_ref[...], b_ref[...])


def _ple_ln(x, p, wg, we, g, b, *, tm=512):
    n, d = x.shape
    vec = pl.BlockSpec((1, d), lambda i: (0, 0))
    return pl.pallas_call(
        _ple_kernel,
        grid=(n // tm,),
        in_specs=[pl.BlockSpec((tm, d), lambda i: (i, 0)), pl.BlockSpec((tm, PLE_DIM), lambda i: (i, 0)),
                  pl.BlockSpec((d, d), lambda i: (0, 0)), pl.BlockSpec((PLE_DIM, d), lambda i: (0, 0)), vec, vec],
        out_specs=pl.BlockSpec((tm, d), lambda i: (i, 0)),
        out_shape=jax.ShapeDtypeStruct((n, d), F32),
        compiler_params=_params("parallel"),
        name="ple_ln",
    )(x, p, wg, we, g, b)


def _rotary_column_order():
    per_head = np.concatenate([np.arange(0, B_HEAD_DIM, 2), np.arange(1, B_HEAD_DIM, 2)])
    return np.concatenate([hh * B_HEAD_DIM + per_head for hh in range(B_HEADS)])


def _mix_in_weight(w):
    perm = _rotary_column_order()
    cols = np.arange(IN_COLS)
    for blk in (3, 4):
        cols[blk * GROUP_WIDTH:(blk + 1) * GROUP_WIDTH] = blk * GROUP_WIDTH + perm
    return w[:, cols].astype(BF16)


def kernel(x, p, positions, ffn1_w_in, ffn1_w_out, w_mix_in, w_mix_out, rel_bias, diff_lambda, diff_norm_g,
           gmlp_ln_g, gmlp_ln_b, gmlp_w_s, gmlp_b_s, hgrn_lb_logits, hgrn_norm_g, ffn2_w_in, ffn2_w_out,
           ple_w_gate, ple_w_proj, ln_g, ln_b):
    bsz, seq, d = x.shape
    n = bsz * seq
    t_attn = min(512, seq)
    xs = x.reshape(n, d)

    pos_b = jnp.broadcast_to(positions.reshape(n, 1), (n, LANES))
    inv = ROPE_BASE ** (-jnp.linspace(0.0, 1.0, B_HEAD_DIM // 2, dtype=F32))
    inv_b = jnp.concatenate([inv, inv]).reshape(1, LANES)
    cos_t, sin_t = _rope_tables(pos_b, inv_b, tr=min(1024, n))
    bias_tiles = _bias_tiles(rel_bias, t_attn)

    for i in range(DEPTH):
        lng = ln_g[i].reshape(4, 1, d)
        lnb = ln_b[i].reshape(4, 1, d)
        xs = _ffn_ln(xs, ffn1_w_in[i].astype(BF16), ffn1_w_out[i].astype(BF16), lng[0], lnb[0])
        h = _in_proj(xs, _mix_in_weight(w_mix_in[i]))
        out_a = _diff_attention(h, bias_tiles, rel_bias, diff_lambda[i], diff_norm_g[i].reshape(1, LANES),
                                bsz=bsz, seq=seq, t=t_attn, layer_idx=i)
        out_b = _retention(h, cos_t, sin_t, bsz=bsz, seq=seq)
        bs_b = jnp.broadcast_to(gmlp_b_s[i][:, :, None], (C_GROUPS, C_CHUNK, LANES))
        out_c = _spatial_gating(h, gmlp_ln_g[i].reshape(1, GROUP_WIDTH), gmlp_ln_b[i].reshape(1, GROUP_WIDTH),
                                gmlp_w_s[i], bs_b)
        out_d = _hgrn2(h, hgrn_lb_logits, hgrn_norm_g[i].reshape(1, GROUP_WIDTH), bsz=bsz, seq=seq, layer_idx=i)
        xs = _out_proj_ln(xs, (out_a, out_b, out_c, out_d),
                          w_mix_out[i].astype(BF16).reshape(4, GROUP_WIDTH, d), lng[1], lnb[1])
        xs = _ffn_ln(xs, ffn2_w_in[i].astype(BF16), ffn2_w_out[i].astype(BF16), lng[2], lnb[2])
        xs = _ple_ln(xs, p[i].reshape(n, PLE_DIM), ple_w_gate[i].astype(BF16), ple_w_proj[i].astype(BF16),
                     lng[3], lnb[3])
    return xs.reshape(bsz, seq, d)
```

```python
import functools
import math

import numpy as np
import jax
import jax.numpy as jnp
from jax import lax
from jax.experimental import pallas as pl
from jax.experimental.pallas import tpu as pltpu

D_MODEL = 2048
DEPTH = 2
PLE_DIM = 256
D_FF = 5632
GROUP_WIDTH = 512

A_HEADS = 4
A_HEAD_DIM = 64
N_BUCKETS = 32
MAX_DISTANCE = 128

B_HEADS = 4
B_HEAD_DIM = 128
B_CHUNK = 128
ROPE_BASE = 10000.0

C_GROUPS = 4
C_CHUNK = 128

D_HEADS = 4
D_CHUNK = 64
D_LEVELS = 6

IN_COLS = 13 * GROUP_WIDTH
ALPHA = (2 * DEPTH) ** 0.25
LN_EPS = 1e-5
MASK_VALUE = -1e30
LB_FLOOR = 1e-30
LOG2E = math.log2(math.e)

LANES = 128
VMEM_LIMIT = 56 * 1024 * 1024
FFN_VMEM_LIMIT = 63 * 1024 * 1024

F32 = jnp.float32
BF16 = jnp.bfloat16


def _dot(a, b):
    return jnp.dot(a, b, preferred_element_type=F32)


def _dot_nt(a, b):
    return lax.dot_general(a, b, (((1,), (1,)), ((), ())), preferred_element_type=F32)


def _sigmoid(x):
    return 1.0 / (1.0 + jnp.exp(-x))


def _layer_norm(y, g, b):
    mu = jnp.mean(y, axis=-1, keepdims=True)
    d = y - mu
    var = jnp.mean(d * d, axis=-1, keepdims=True)
    return d * lax.rsqrt(var + LN_EPS) * g + b


def _params(*sem, vmem=VMEM_LIMIT):
    return pltpu.CompilerParams(dimension_semantics=sem, vmem_limit_bytes=vmem)


def _ffn_kernel(x_ref, wg_ref, wu_ref, wo_ref, g_ref, b_ref, o_ref, xb_ref):
    j = pl.program_id(1)

    @pl.when(j == 0)
    def _():
        xb_ref[...] = x_ref[...].astype(BF16)
        o_ref[...] = jnp.zeros_like(o_ref)

    xb = xb_ref[...]
    hg = _dot(xb, wg_ref[...])
    hu = _dot(xb, wu_ref[...])
    a = (hg * _sigmoid(hg) * hu).astype(BF16)
    o_ref[...] += _dot(a, wo_ref[...])

    @pl.when(j == pl.num_programs(1) - 1)
    def _():
        y = ALPHA * x_ref[...] + 0.5 * o_ref[...]
        o_ref[...] = _layer_norm(y, g_ref[...], b_ref[...])


def _ffn_ln(x, w_in, w_out, g, b, *, tm=1024, tf=512):
    n, d = x.shape
    f = w_out.shape[0]
    nf = f // tf
    return pl.pallas_call(
        _ffn_kernel,
        grid=(n // tm, nf),
        in_specs=[
            pl.BlockSpec((tm, d), lambda i, j: (i, 0)),
            pl.BlockSpec((d, tf), lambda i, j: (0, j)),
            pl.BlockSpec((d, tf), lambda i, j: (0, j + nf)),
            pl.BlockSpec((tf, d), lambda i, j: (j, 0)),
            pl.BlockSpec((1, d), lambda i, j: (0, 0)),
            pl.BlockSpec((1, d), lambda i, j: (0, 0)),
        ],
        out_specs=pl.BlockSpec((tm, d), lambda i, j: (i, 0)),
        out_shape=jax.ShapeDtypeStruct((n, d), F32),
        scratch_shapes=[pltpu.VMEM((tm, d), BF16)],
        compiler_params=_params("parallel", "arbitrary", vmem=FFN_VMEM_LIMIT),
        name="ffn_ln",
    )(x, w_in, w_in, w_out, g, b)


def _in_proj_kernel(x_ref, w_ref, o_ref, xb_ref):
    @pl.when(pl.program_id(1) == 0)
    def _():
        xb_ref[...] = x_ref[...].astype(BF16)

    o_ref[...] = _dot(xb_ref[...], w_ref[...])


def _in_proj(x, w, *, tm=1024, tn=512):
    n, d = x.shape
    c = w.shape[1]
    return pl.pallas_call(
        _in_proj_kernel,
        grid=(n // tm, c // tn),
        in_specs=[
            pl.BlockSpec((tm, d), lambda i, j: (i, 0)),
            pl.BlockSpec((d, tn), lambda i, j: (0, j)),
        ],
        out_specs=pl.BlockSpec((tm, tn), lambda i, j: (i, j)),
        out_shape=jax.ShapeDtypeStruct((n, c), F32),
        scratch_shapes=[pltpu.VMEM((tm, d), BF16)],
        compiler_params=_params("parallel", "arbitrary"),
        name="in_proj",
    )(x, w)


def _bias_tiles_kernel(rb_ref, o_ref, *, t):
    h = pl.program_id(0)
    j = lax.broadcasted_iota(jnp.int32, (t, t), 0)
    i = lax.broadcasted_iota(jnp.int32, (t, t), 1)
    max_exact = N_BUCKETS // 2
    for sel in range(2):
        n = jnp.maximum(i - j + sel * t, 0)
        nf = jnp.maximum(n, 1).astype(F32)
        large = max_exact + (jnp.log(nf / max_exact) / math.log(MAX_DISTANCE / max_exact)
                             * (N_BUCKETS - max_exact)).astype(jnp.int32)
        large = jnp.minimum(large, N_BUCKETS - 1)
        bucket = jnp.where(n < max_exact, n, large)
        val = jnp.zeros((t, t), F32)
        for bkt in range(N_BUCKETS):
            val = jnp.where(bucket == bkt, rb_ref[bkt, h] * LOG2E, val)
        if sel == 0:
            val = jnp.where(i >= j, val, MASK_VALUE)
        o_ref[0, sel] = val


def _bias_tiles(rel_bias, t):
    return pl.pallas_call(
        functools.partial(_bias_tiles_kernel, t=t),
        grid=(A_HEADS,),
        in_specs=[pl.BlockSpec(memory_space=pltpu.SMEM)],
        out_specs=pl.BlockSpec((1, 2, t, t), lambda h: (h, 0, 0, 0)),
        out_shape=jax.ShapeDtypeStruct((A_HEADS, 2, t, t), F32),
        compiler_params=_params("arbitrary"),
        name="t5_bias_tiles",
    )(rel_bias)


def _attn_kernel(q_ref, k_ref, v_ref, bt_ref, rb_ref, lam_ref, ng_ref, o_ref,
                 kb_ref, vt_ref, qst_ref, sa_ref, sb_ref, m_ref, l_ref, acc_ref, *, t, lam_init):
    h = pl.program_id(1)
    i = pl.program_id(2)

    @pl.when(i == 0)
    def _():
        def fill(c, carry):
            rows = pl.ds(pl.multiple_of(c * t, t), t)
            kb_ref[c] = k_ref[rows, :].astype(BF16)
            vt_ref[c] = v_ref[rows, :].T.astype(BF16)
            return carry

        lax.fori_loop(0, kb_ref.shape[0], fill, 0)

    qt = (q_ref[...] * (A_HEAD_DIM ** -0.5 * LOG2E)).T
    feat = lax.broadcasted_iota(jnp.int32, qt.shape, 0)
    qst_ref[:, 0:t] = jnp.where(feat < A_HEAD_DIM, qt, 0.0).astype(BF16)
    qst_ref[:, t:2 * t] = jnp.where(feat >= A_HEAD_DIM, qt, 0.0).astype(BF16)
    m_ref[...] = jnp.full(m_ref.shape, MASK_VALUE, F32)
    l_ref[...] = jnp.zeros_like(l_ref)
    acc_ref[...] = jnp.zeros_like(acc_ref)

    def scores(kj, s2_ref):
        s2_ref[...] = _dot(kb_ref[kj], qst_ref[...])

    def step(kj, s2_ref, tile_sel, const_bias):
        vt = vt_ref[kj]
        for c in range(2):
            s = s2_ref[:, c * t:(c + 1) * t]
            if tile_sel is not None:
                s = s + bt_ref[0, tile_sel]
            m_old = m_ref[c]
            smax = jnp.max(s, axis=0, keepdims=True)
            if const_bias is not None:
                smax = smax + const_bias
            m_new = jnp.maximum(m_old, smax)
            shift = m_new if const_bias is None else m_new - const_bias
            p = jnp.exp2(s - shift)
            corr = jnp.exp2(m_old - m_new)
            l_ref[c] = corr * l_ref[c] + jnp.sum(p, axis=0, keepdims=True)
            acc_ref[c] = corr * acc_ref[c] + _dot(vt, p.astype(BF16))
            m_ref[c] = m_new

    far_bias = rb_ref[N_BUCKETS - 1, h] * LOG2E

    n_far = i - 1
    odd_far = jnp.logical_and(i >= 2, n_far % 2 == 1)
    scores(0, sa_ref)

    def far_pair(pj, carry):
        scores(2 * pj + 1, sb_ref)
        step(2 * pj, sa_ref, None, far_bias)
        scores(2 * pj + 2, sa_ref)
        step(2 * pj + 1, sb_ref, None, far_bias)
        return carry

    lax.fori_loop(0, jnp.maximum(n_far, 0) // 2, far_pair, 0)

    @pl.when(odd_far)
    def _():
        scores(i - 1, sb_ref)
        step(i - 2, sa_ref, None, far_bias)
        scores(i, sa_ref)
        step(i - 1, sb_ref, 1, None)
        step(i, sa_ref, 0, None)

    @pl.when(jnp.logical_and(i >= 1, jnp.logical_not(odd_far)))
    def _():
        scores(i, sb_ref)
        step(i - 1, sa_ref, 1, None)
        step(i, sb_ref, 0, None)

    @pl.when(i == 0)
    def _():
        step(0, sa_ref, 0, None)

    lmb = lam_ref[...]
    lam = (jnp.exp(jnp.sum(lmb[0:1] * lmb[1:2], axis=1, keepdims=True))
           - jnp.exp(jnp.sum(lmb[2:3] * lmb[3:4], axis=1, keepdims=True)) + lam_init)
    o = (acc_ref[0] / l_ref[0] - lam * (acc_ref[1] / l_ref[1])).T
    o = o * lax.rsqrt(jnp.mean(o * o, axis=-1, keepdims=True) + LN_EPS) * ng_ref[...]
    o_ref[...] = (o * (1.0 - lam_init)).astype(o_ref.dtype)


def _diff_attention(h, bias_tiles, rel_bias, diff_lambda, norm_g, *, bsz, seq, t, layer_idx):
    n = bsz * seq
    nq = seq // t
    lam_init = 0.8 - 0.6 * math.exp(-0.3 * layer_idx)
    kcol = GROUP_WIDTH // LANES
    return pl.pallas_call(
        functools.partial(_attn_kernel, t=t, lam_init=lam_init),
        grid=(bsz, A_HEADS, nq),
        in_specs=[
            pl.BlockSpec((t, LANES), lambda b, hh, i: (b * nq + i, hh)),
            pl.BlockSpec((seq, LANES), lambda b, hh, i: (b, kcol + hh)),
            pl.BlockSpec((seq, LANES), lambda b, hh, i: (b, 2 * kcol + hh)),
            pl.BlockSpec((1, 2, t, t), lambda b, hh, i: (hh, 0, 0, 0)),
            pl.BlockSpec(memory_space=pltpu.SMEM),
            pl.BlockSpec((4, A_HEAD_DIM), lambda b, hh, i: (0, 0)),
            pl.BlockSpec((1, LANES), lambda b, hh, i: (0, 0)),
        ],
        out_specs=pl.BlockSpec((t, LANES), lambda b, hh, i: (b * nq + i, hh)),
        out_shape=jax.ShapeDtypeStruct((n, GROUP_WIDTH), BF16),
        scratch_shapes=[
            pltpu.VMEM((nq, t, LANES), BF16), pltpu.VMEM((nq, LANES, t), BF16),
            pltpu.VMEM((LANES, 2 * t), BF16), pltpu.VMEM((t, 2 * t), F32), pltpu.VMEM((t, 2 * t), F32),
            pltpu.VMEM((2, 1, t), F32), pltpu.VMEM((2, 1, t), F32), pltpu.VMEM((2, LANES, t), F32),
        ],
        compiler_params=_params("parallel", "parallel", "arbitrary"),
        name="diff_attention",
    )(h, h, h, bias_tiles, rel_bias, diff_lambda, norm_g)


def _rope_kernel(pos_ref, inv_ref, cos_ref, sin_ref):
    ang = pos_ref[...].astype(F32) * inv_ref[...]
    lane = lax.broadcasted_iota(jnp.int32, ang.shape, 1)
    cos_ref[...] = jnp.cos(ang)
    sn = jnp.sin(ang)
    sin_ref[...] = jnp.where(lane < B_HEAD_DIM // 2, -sn, sn)


def _rope_tables(pos_b, inv_b, *, tr=1024):
    n = pos_b.shape[0]
    return pl.pallas_call(
        _rope_kernel,
        grid=(n // tr,),
        in_specs=[pl.BlockSpec((tr, LANES), lambda i: (i, 0)), pl.BlockSpec((1, LANES), lambda i: (0, 0))],
        out_specs=[pl.BlockSpec((tr, LANES), lambda i: (i, 0))] * 2,
        out_shape=[jax.ShapeDtypeStruct((n, LANES), F32)] * 2,
        compiler_params=_params("parallel"),
        name="rope_tables",
    )(pos_b, inv_b)


def _retention_consts():
    log_g = np.log(1.0 - 2.0 ** (-5.0 - np.arange(B_HEADS, dtype=np.float64)))
    j = np.arange(B_CHUNK, dtype=np.float64)
    diff = j[:, None] - j[None, :]
    decay_mask = np.where(diff >= 0, np.exp(log_g[:, None, None] * np.maximum(diff, 0.0)), 0.0)
    q_dec = np.exp(log_g[:, None] * (j[None, :] + 1.0))
    k_dec = np.exp(log_g[:, None] * (B_CHUNK - 1.0 - j[None, :]))
    chunk_dec = np.exp(log_g * B_CHUNK)
    bc = lambda a: np.ascontiguousarray(np.broadcast_to(a[:, :, None], (B_HEADS, B_CHUNK, LANES)))
    return (decay_mask.astype(np.float32), bc(q_dec).astype(np.float32), bc(k_dec).astype(np.float32),
            [float(c) for c in chunk_dec])


def _retention_kernel(q_ref, k_ref, v_ref, g_ref, cos_ref, sin_ref, dm_ref, qd_ref, kd_ref, o_ref, st_ref,
                      *, chunk_dec):
    @pl.when(pl.program_id(1) == 0)
    def _():
        st_ref[...] = jnp.zeros_like(st_ref)

    tr = q_ref.shape[0]
    half = B_HEAD_DIM // 2
    for c in range(tr // B_CHUNK):
        rows = slice(c * B_CHUNK, (c + 1) * B_CHUNK)
        cos = cos_ref[rows, :]
        sin = sin_ref[rows, :]
        for hh in range(B_HEADS):
            cols = slice(hh * B_HEAD_DIM, (hh + 1) * B_HEAD_DIM)
            q = q_ref[rows, cols]
            k = k_ref[rows, cols]
            q = q * cos + pltpu.roll(q, half, 1) * sin
            k = (k * cos + pltpu.roll(k, half, 1) * sin) * (B_HEAD_DIM ** -0.5)
            qb = q.astype(BF16)
            vb = v_ref[rows, cols].astype(BF16)
            scores = _dot_nt(qb, k.astype(BF16)) * dm_ref[hh]
            st = st_ref[hh]
            o = _dot(scores.astype(BF16), vb) + _dot(qb, st.astype(BF16)) * qd_ref[hh]
            st_ref[hh] = st * chunk_dec[hh] + _dot((k * kd_ref[hh]).T.astype(BF16), vb)
            mu = jnp.mean(o, axis=-1, keepdims=True)
            d = o - mu
            var = jnp.mean(d * d, axis=-1, keepdims=True)
            gate = g_ref[rows, cols]
            o_ref[rows, cols] = (d * lax.rsqrt(var + LN_EPS) * (gate * _sigmoid(gate))).astype(o_ref.dtype)


def _retention(h, cos_t, sin_t, *, bsz, seq, tr=512):
    n = bsz * seq
    ns = seq // tr
    dm, qd, kd, chunk_dec = _retention_consts()
    row = lambda b, s: b * ns + s
    const3 = pl.BlockSpec((B_HEADS, B_CHUNK, LANES), lambda b, s: (0, 0, 0))
    return pl.pallas_call(
        functools.partial(_retention_kernel, chunk_dec=chunk_dec),
        grid=(bsz, ns),
        in_specs=[
            pl.BlockSpec((tr, GROUP_WIDTH), lambda b, s: (row(b, s), 3)),
            pl.BlockSpec((tr, GROUP_WIDTH), lambda b, s: (row(b, s), 4)),
            pl.BlockSpec((tr, GROUP_WIDTH), lambda b, s: (row(b, s), 5)),
            pl.BlockSpec((tr, GROUP_WIDTH), lambda b, s: (row(b, s), 6)),
            pl.BlockSpec((tr, LANES), lambda b, s: (row(b, s), 0)),
            pl.BlockSpec((tr, LANES), lambda b, s: (row(b, s), 0)),
            const3, const3, const3,
        ],
        out_specs=pl.BlockSpec((tr, GROUP_WIDTH), lambda b, s: (row(b, s), 0)),
        out_shape=jax.ShapeDtypeStruct((n, GROUP_WIDTH), BF16),
        scratch_shapes=[pltpu.VMEM((B_HEADS, B_HEAD_DIM, B_HEAD_DIM), F32)],
        compiler_params=_params("parallel", "arbitrary"),
        name="retention",
    )(h, h, h, h, cos_t, sin_t, jnp.asarray(dm), jnp.asarray(qd), jnp.asarray(kd))


def _gelu(x):
    return 0.5 * x * (1.0 + lax.erf(x * (0.5 ** 0.5)))


def _gmlp_kernel(u_ref, v_ref, g_ref, b_ref, ws_ref, bs_ref, o_ref):
    tr = u_ref.shape[0]
    v = _layer_norm(_gelu(v_ref[...]), g_ref[...], b_ref[...]).astype(BF16)
    ti = lax.broadcasted_iota(jnp.int32, (C_CHUNK, C_CHUNK), 0)
    si = lax.broadcasted_iota(jnp.int32, (C_CHUNK, C_CHUNK), 1)
    for gi in range(C_GROUPS):
        cols = slice(gi * LANES, (gi + 1) * LANES)
        w = jnp.where(ti >= si, ws_ref[gi], 0.0).astype(BF16)
        for c in range(tr // C_CHUNK):
            rows = slice(c * C_CHUNK, (c + 1) * C_CHUNK)
            mixed = _dot(w, v[rows, cols]) + bs_ref[gi]
            o_ref[rows, cols] = (_gelu(u_ref[rows, cols]) * mixed).astype(o_ref.dtype)


def _spatial_gating(h, ln_g, ln_b, w_s, bs_b, *, tr=512):
    n = h.shape[0]
    const3 = pl.BlockSpec((C_GROUPS, C_CHUNK, LANES), lambda i: (0, 0, 0))
    vec = pl.BlockSpec((1, GROUP_WIDTH), lambda i: (0, 0))
    return pl.pallas_call(
        _gmlp_kernel,
        grid=(n // tr,),
        in_specs=[
            pl.BlockSpec((tr, GROUP_WIDTH), lambda i: (i, 7)),
            pl.BlockSpec((tr, GROUP_WIDTH), lambda i: (i, 8)),
            vec, vec, const3, const3,
        ],
        out_specs=pl.BlockSpec((tr, GROUP_WIDTH), lambda i: (i, 0)),
        out_shape=jax.ShapeDtypeStruct((n, GROUP_WIDTH), BF16),
        compiler_params=_params("parallel"),
        name="spatial_gating",
    )(h, h, ln_g, ln_b, w_s, bs_b)


def _hgrn_consts():
    c = D_CHUNK
    t = np.arange(c)[:, None]
    r = np.arange(c)[None, :]
    ranges = [r <= t, r > t]
    masks = [r == t]
    m = c // 2
    while m >= 1:
        ref = (t // (2 * m)) * 2 * m + m - 1
        second = (t % (2 * m)) >= m
        ranges.append(np.where(second, (r > ref) & (r <= t), (r > t) & (r <= ref)))
        masks.append(((t // (2 * m)) == (r // (2 * m))) & second & ((r % (2 * m)) < m))
        m //= 2
    return (np.concatenate(ranges, axis=0).astype(np.float32), np.stack(masks).astype(np.float32))


def _hgrn_kernel(q_ref, z_ref, i_ref, g_ref, lbl_ref, ng_ref, rs_ref, pm_ref, o_ref, st_ref, oc_ref, *, layer_idx):
    @pl.when(pl.program_id(1) == 0)
    def _():
        st_ref[...] = jnp.zeros_like(st_ref)

    tr = q_ref.shape[0]
    cs = D_CHUNK

    logits = lbl_ref[...]
    e = jnp.exp(logits - jnp.max(logits, axis=0, keepdims=True))
    soft = e / jnp.sum(e, axis=0, keepdims=True)
    lb = jnp.sum(soft[0:layer_idx + 1], axis=0, keepdims=True) - soft[0:1]
    lb = jnp.maximum(lb, LB_FLOOR)

    for c in range(tr // cs):
        rows = slice(c * cs, (c + 1) * cs)
        z = z_ref[rows, :]
        ez = jnp.exp(-jnp.abs(z))
        r = 1.0 / (1.0 + ez)
        sig_pos = jnp.where(z >= 0, r, ez * r)
        sig_neg = jnp.where(z >= 0, ez * r, r)
        log2_f = jnp.log(sig_pos + lb * sig_neg) * LOG2E
        key_all = (1.0 - lb) * sig_neg
        hi = log2_f.astype(BF16)
        lo = (log2_f - hi.astype(F32)).astype(BF16)
        e2 = _dot(rs_ref[...], jnp.concatenate([hi, lo], axis=1))
        e_all = e2[:, 0:GROUP_WIDTH] + e2[:, GROUP_WIDTH:2 * GROUP_WIDTH]
        for hh in range(D_HEADS):
            cols = slice(hh * LANES, (hh + 1) * LANES)
            q = q_ref[rows, cols]
            k = key_all[:, cols]
            vb = i_ref[rows, cols].astype(BF16)
            b = e_all[0:cs, cols]
            rem = e_all[cs:2 * cs, cols]
            st = st_ref[hh]
            o = _dot_nt((q * jnp.exp2(b)).astype(BF16), st.astype(BF16))
            a = jnp.where(pm_ref[0] > 0, _dot_nt(q.astype(BF16), k.astype(BF16)), 0.0)
            for lv in range(D_LEVELS):
                w = jnp.exp2(e_all[(2 + lv) * cs:(3 + lv) * cs, cols])
                a = a + jnp.where(pm_ref[1 + lv] > 0, _dot_nt((q * w).astype(BF16), (k * w).astype(BF16)), 0.0)
            oc_ref[rows, cols] = o + _dot(a.astype(BF16), vb)
            kdec = (k * jnp.exp2(rem)).astype(BF16)
            st_ref[hh] = st * jnp.exp2(b[cs - 1:cs, :]) + _dot(i_ref[rows, cols].T.astype(BF16), kdec)

    o = oc_ref[...]
    o = o * lax.rsqrt(jnp.mean(o * o, axis=-1, keepdims=True) + LN_EPS) * ng_ref[...]
    gate = g_ref[...]
    o_ref[...] = (o * (gate * _sigmoid(gate))).astype(o_ref.dtype)


def _hgrn2(h, lb_logits, norm_g, *, bsz, seq, layer_idx, tr=256):
    n = bsz * seq
    ns = seq // tr
    row = lambda b, s: b * ns + s
    range_sums, pair_masks = _hgrn_consts()
    return pl.pallas_call(
        functools.partial(_hgrn_kernel, layer_idx=layer_idx),
        grid=(bsz, ns),
        in_specs=[
            pl.BlockSpec((tr, GROUP_WIDTH), lambda b, s: (row(b, s), 9)),
            pl.BlockSpec((tr, GROUP_WIDTH), lambda b, s: (row(b, s), 10)),
            pl.BlockSpec((tr, GROUP_WIDTH), lambda b, s: (row(b, s), 11)),
            pl.BlockSpec((tr, GROUP_WIDTH), lambda b, s: (row(b, s), 12)),
            pl.BlockSpec((DEPTH, GROUP_WIDTH), lambda b, s: (0, 0)),
            pl.BlockSpec((1, GROUP_WIDTH), lambda b, s: (0, 0)),
            pl.BlockSpec(range_sums.shape, lambda b, s: (0, 0)),
            pl.BlockSpec(pair_masks.shape, lambda b, s: (0, 0, 0)),
        ],
        out_specs=pl.BlockSpec((tr, GROUP_WIDTH), lambda b, s: (row(b, s), 0)),
        out_shape=jax.ShapeDtypeStruct((n, GROUP_WIDTH), BF16),
        scratch_shapes=[pltpu.VMEM((D_HEADS, LANES, LANES), F32), pltpu.VMEM((tr, GROUP_WIDTH), F32)],
        compiler_params=_params("parallel", "arbitrary"),
        name="hgrn2",
    )(h, h, h, h, lb_logits, norm_g, jnp.asarray(range_sums, BF16), jnp.asarray(pair_masks))


def _out_proj_kernel(x_ref, a_ref, b_ref, c_ref, d_ref, w_ref, g_ref, bb_ref, o_ref):
    acc = _dot(a_ref[...], w_ref[0])
    acc += _dot(b_ref[...], w_ref[1])
    acc += _dot(c_ref[...], w_ref[2])
    acc += _dot(d_ref[...], w_ref[3])
    o_ref[...] = _layer_norm(ALPHA * x_ref[...] + acc, g_ref[...], bb_ref[...])


def _out_proj_ln(x, parts, w, g, b, *, tm=512):
    n, d = x.shape
    part = pl.BlockSpec((tm, GROUP_WIDTH), lambda i: (i, 0))
    vec = pl.BlockSpec((1, d), lambda i: (0, 0))
    return pl.pallas_call(
        _out_proj_kernel,
        grid=(n // tm,),
        in_specs=[pl.BlockSpec((tm, d), lambda i: (i, 0)), part, part, part, part,
                  pl.BlockSpec((4, GROUP_WIDTH, d), lambda i: (0, 0, 0)), vec, vec],
        out_specs=pl.BlockSpec((tm, d), lambda i: (i, 0)),
        out_shape=jax.ShapeDtypeStruct((n, d), F32),
        compiler_params=_params("parallel"),
        name="out_proj_ln",
    )(x, *parts, w, g, b)


def _ple_kernel(x_ref, p_ref, wg_ref, we_ref, g_ref, b_ref, o_ref):
    x = x_ref[...]
    gate = _sigmoid(_dot(x.astype(BF16), wg_ref[...]))
    emb = _dot(p_ref[...].astype(BF16), we_ref[...])
    o_ref[...] = _layer_norm(ALPHA * x + gate * emb, g_ref[...], b_ref[...])


def _ple_ln(x, p, wg, we, g, b, *, tm=512):
    n, d = x.shape
    vec = pl.BlockSpec((1, d), lambda i: (0, 0))
    return pl.pallas_call(
        _ple_kernel,
        grid=(n // tm,),
        in_specs=[pl.BlockSpec((tm, d), lambda i: (i, 0)), pl.BlockSpec((tm, PLE_DIM), lambda i: (i, 0)),
                  pl.BlockSpec((d, d), lambda i: (0, 0)), pl.BlockSpec((PLE_DIM, d), lambda i: (0, 0)), vec, vec],
        out_specs=pl.BlockSpec((tm, d), lambda i: (i, 0)),
        out_shape=jax.ShapeDtypeStruct((n, d), F32),
        compiler_params=_params("parallel"),
        name="ple_ln",
    )(x, p, wg, we, g, b)


def _rotary_column_order():
    per_head = np.concatenate([np.arange(0, B_HEAD_DIM, 2), np.arange(1, B_HEAD_DIM, 2)])
    return np.concatenate([hh * B_HEAD_DIM + per_head for hh in range(B_HEADS)])


def _mix_in_weight(w):
    perm = _rotary_column_order()
    cols = np.arange(IN_COLS)
    for blk in (3, 4):
        cols[blk * GROUP_WIDTH:(blk + 1) * GROUP_WIDTH] = blk * GROUP_WIDTH + perm
    return w[:, cols].astype(BF16)


def kernel(x, p, positions, ffn1_w_in, ffn1_w_out, w_mix_in, w_mix_out, rel_bias, diff_lambda, diff_norm_g,
           gmlp_ln_g, gmlp_ln_b, gmlp_w_s, gmlp_b_s, hgrn_lb_logits, hgrn_norm_g, ffn2_w_in, ffn2_w_out,
           ple_w_gate, ple_w_proj, ln_g, ln_b):
    bsz, seq, d = x.shape
    n = bsz * seq
    t_attn = min(512, seq)
    xs = x.reshape(n, d)

    pos_b = jnp.broadcast_to(positions.reshape(n, 1), (n, LANES))
    inv = ROPE_BASE ** (-jnp.linspace(0.0, 1.0, B_HEAD_DIM // 2, dtype=F32))
    inv_b = jnp.concatenate([inv, inv]).reshape(1, LANES)
    cos_t, sin_t = _rope_tables(pos_b, inv_b, tr=min(1024, n))
    bias_tiles = _bias_tiles(rel_bias, t_attn)

    for i in range(DEPTH):
        lng = ln_g[i].reshape(4, 1, d)
        lnb = ln_b[i].reshape(4, 1, d)
        xs = _ffn_ln(xs, ffn1_w_in[i].astype(BF16), ffn1_w_out[i].astype(BF16), lng[0], lnb[0])
        h = _in_proj(xs, _mix_in_weight(w_mix_in[i]))
        out_a = _diff_attention(h, bias_tiles, rel_bias, diff_lambda[i], diff_norm_g[i].reshape(1, LANES),
                                bsz=bsz, seq=seq, t=t_attn, layer_idx=i)
        out_b = _retention(h, cos_t, sin_t, bsz=bsz, seq=seq)
        bs_b = jnp.broadcast_to(gmlp_b_s[i][:, :, None], (C_GROUPS, C_CHUNK, LANES))
        out_c = _spatial_gating(h, gmlp_ln_g[i].reshape(1, GROUP_WIDTH), gmlp_ln_b[i].reshape(1, GROUP_WIDTH),
                                gmlp_w_s[i], bs_b)
        out_d = _hgrn2(h, hgrn_lb_logits, hgrn_norm_g[i].reshape(1, GROUP_WIDTH), bsz=bsz, seq=seq, layer_idx=i)
        xs = _out_proj_ln(xs, (out_a, out_b, out_c, out_d),
                          w_mix_out[i].astype(BF16).reshape(4, GROUP_WIDTH, d), lng[1], lnb[1])
        xs = _ffn_ln(xs, ffn2_w_in[i].astype(BF16), ffn2_w_out[i].astype(BF16), lng[2], lnb[2])
        xs = _ple_ln(xs, p[i].reshape(n, PLE_DIM), ple_w_gate[i].astype(BF16), ple_w_proj[i].astype(BF16),
                     lng[3], lnb[3])
    return xs.reshape(bsz, seq, d)
```

```python
import functools
import math

import numpy as np
import jax
import jax.numpy as jnp
from jax import lax
from jax.experimental import pallas as pl
from jax.experimental.pallas import tpu as pltpu

D_MODEL = 2048
DEPTH = 2
PLE_DIM = 256
D_FF = 5632
GROUP_WIDTH = 512

A_HEADS = 4
A_HEAD_DIM = 64
N_BUCKETS = 32
MAX_DISTANCE = 128

B_HEADS = 4
B_HEAD_DIM = 128
B_CHUNK = 128
ROPE_BASE = 10000.0

C_GROUPS = 4
C_CHUNK = 128

D_HEADS = 4
D_CHUNK = 64
D_LEVELS = 6

IN_COLS = 13 * GROUP_WIDTH
ALPHA = (2 * DEPTH) ** 0.25
LN_EPS = 1e-5
MASK_VALUE = -1e30
LB_FLOOR = 1e-30
LOG2E = math.log2(math.e)

LANES = 128
VMEM_LIMIT = 56 * 1024 * 1024
FFN_VMEM_LIMIT = 63 * 1024 * 1024

F32 = jnp.float32
BF16 = jnp.bfloat16


def _dot(a, b):
    return jnp.dot(a, b, preferred_element_type=F32)


def _dot_nt(a, b):
    return lax.dot_general(a, b, (((1,), (1,)), ((), ())), preferred_element_type=F32)


def _sigmoid(x):
    return 1.0 / (1.0 + jnp.exp(-x))


def _layer_norm(y, g, b):
    mu = jnp.mean(y, axis=-1, keepdims=True)
    d = y - mu
    var = jnp.mean(d * d, axis=-1, keepdims=True)
    return d * lax.rsqrt(var + LN_EPS) * g + b


def _params(*sem, vmem=VMEM_LIMIT):
    return pltpu.CompilerParams(dimension_semantics=sem, vmem_limit_bytes=vmem)


def _ffn_kernel(x_ref, wg_ref, wu_ref, wo_ref, g_ref, b_ref, o_ref, xb_ref):
    j = pl.program_id(1)

    @pl.when(j == 0)
    def _():
        xb_ref[...] = x_ref[...].astype(BF16)
        o_ref[...] = jnp.zeros_like(o_ref)

    xb = xb_ref[...]
    hg = _dot(xb, wg_ref[...])
    hu = _dot(xb, wu_ref[...])
    a = (hg * _sigmoid(hg) * hu).astype(BF16)
    o_ref[...] += _dot(a, wo_ref[...])

    @pl.when(j == pl.num_programs(1) - 1)
    def _():
        y = ALPHA * x_ref[...] + 0.5 * o_ref[...]
        o_ref[...] = _layer_norm(y, g_ref[...], b_ref[...])


def _ffn_ln(x, w_in, w_out, g, b, *, tm=1024, tf=512):
    n, d = x.shape
    f = w_out.shape[0]
    nf = f // tf
    return pl.pallas_call(
        _ffn_kernel,
        grid=(n // tm, nf),
        in_specs=[
            pl.BlockSpec((tm, d), lambda i, j: (i, 0)),
            pl.BlockSpec((d, tf), lambda i, j: (0, j)),
            pl.BlockSpec((d, tf), lambda i, j: (0, j + nf)),
            pl.BlockSpec((tf, d), lambda i, j: (j, 0)),
            pl.BlockSpec((1, d), lambda i, j: (0, 0)),
            pl.BlockSpec((1, d), lambda i, j: (0, 0)),
        ],
        out_specs=pl.BlockSpec((tm, d), lambda i, j: (i, 0)),
        out_shape=jax.ShapeDtypeStruct((n, d), F32),
        scratch_shapes=[pltpu.VMEM((tm, d), BF16)],
        compiler_params=_params("parallel", "arbitrary", vmem=FFN_VMEM_LIMIT),
        name="ffn_ln",
    )(x, w_in, w_in, w_out, g, b)


def _in_proj_kernel(x_ref, w_ref, o_ref, xb_ref):
    @pl.when(pl.program_id(1) == 0)
    def _():
        xb_ref[...] = x_ref[...].astype(BF16)

    o_ref[...] = _dot(xb_ref[...], w_ref[...]).astype(o_ref.dtype)


def _in_proj(x, w, *, tm=1024, tn=512):
    n, d = x.shape
    c = w.shape[1]
    return pl.pallas_call(
        _in_proj_kernel,
        grid=(n // tm, c // tn),
        in_specs=[
            pl.BlockSpec((tm, d), lambda i, j: (i, 0)),
            pl.BlockSpec((d, tn), lambda i, j: (0, j)),
        ],
        out_specs=pl.BlockSpec((tm, tn), lambda i, j: (i, j)),
        out_shape=jax.ShapeDtypeStruct((n, c), BF16),
        scratch_shapes=[pltpu.VMEM((tm, d), BF16)],
        compiler_params=_params("parallel", "arbitrary"),
        name="in_proj",
    )(x, w)


def _bias_tiles_kernel(rb_ref, o_ref, *, t):
    h = pl.program_id(0)
    j = lax.broadcasted_iota(jnp.int32, (t, t), 0)
    i = lax.broadcasted_iota(jnp.int32, (t, t), 1)
    max_exact = N_BUCKETS // 2
    for sel in range(2):
        n = jnp.maximum(i - j + sel * t, 0)
        nf = jnp.maximum(n, 1).astype(F32)
        large = max_exact + (jnp.log(nf / max_exact) / math.log(MAX_DISTANCE / max_exact)
                             * (N_BUCKETS - max_exact)).astype(jnp.int32)
        large = jnp.minimum(large, N_BUCKETS - 1)
        bucket = jnp.where(n < max_exact, n, large)
        val = jnp.zeros((t, t), F32)
        for bkt in range(N_BUCKETS):
            val = jnp.where(bucket == bkt, rb_ref[bkt, h] * LOG2E, val)
        if sel == 0:
            val = jnp.where(i >= j, val, MASK_VALUE)
        o_ref[0, sel] = val


def _bias_tiles(rel_bias, t):
    return pl.pallas_call(
        functools.partial(_bias_tiles_kernel, t=t),
        grid=(A_HEADS,),
        in_specs=[pl.BlockSpec(memory_space=pltpu.SMEM)],
        out_specs=pl.BlockSpec((1, 2, t, t), lambda h: (h, 0, 0, 0)),
        out_shape=jax.ShapeDtypeStruct((A_HEADS, 2, t, t), F32),
        compiler_params=_params("arbitrary"),
        name="t5_bias_tiles",
    )(rel_bias)


def _attn_kernel(q_ref, k_ref, v_ref, bt_ref, rb_ref, lam_ref, ng_ref, o_ref,
                 vt_ref, qst_ref, sa_ref, sb_ref, m_ref, l_ref, acc_ref, *, t, lam_init):
    h = pl.program_id(1)
    i = pl.program_id(2)

    @pl.when(i == 0)
    def _():
        def fill(c, carry):
            rows = pl.ds(pl.multiple_of(c * t, t), t)
            vt_ref[c] = v_ref[rows, :].astype(F32).T.astype(BF16)
            return carry

        lax.fori_loop(0, vt_ref.shape[0], fill, 0)

    qt = (q_ref[...].astype(F32) * (A_HEAD_DIM ** -0.5 * LOG2E)).T
    feat = lax.broadcasted_iota(jnp.int32, qt.shape, 0)
    qst_ref[:, 0:t] = jnp.where(feat < A_HEAD_DIM, qt, 0.0).astype(BF16)
    qst_ref[:, t:2 * t] = jnp.where(feat >= A_HEAD_DIM, qt, 0.0).astype(BF16)
    m_ref[...] = jnp.full(m_ref.shape, MASK_VALUE, F32)
    l_ref[...] = jnp.zeros_like(l_ref)
    acc_ref[...] = jnp.zeros_like(acc_ref)

    def scores(kj, s2_ref):
        keys = k_ref[pl.ds(pl.multiple_of(kj * t, t), t), :]
        s2_ref[...] = _dot(keys, qst_ref[...])

    def step(kj, s2_ref, tile_sel, const_bias):
        vt = vt_ref[kj]
        for c in range(2):
            s = s2_ref[:, c * t:(c + 1) * t]
            if tile_sel is not None:
                s = s + bt_ref[0, tile_sel]
            m_old = m_ref[c]
            smax = jnp.max(s, axis=0, keepdims=True)
            if const_bias is not None:
                smax = smax + const_bias
            m_new = jnp.maximum(m_old, smax)
            shift = m_new if const_bias is None else m_new - const_bias
            p = jnp.exp2(s - shift)
            corr = jnp.exp2(m_old - m_new)
            l_ref[c] = corr * l_ref[c] + jnp.sum(p, axis=0, keepdims=True)
            acc_ref[c] = corr * acc_ref[c] + _dot(vt, p.astype(BF16))
            m_ref[c] = m_new

    far_bias = rb_ref[N_BUCKETS - 1, h] * LOG2E

    n_far = i - 1
    odd_far = jnp.logical_and(i >= 2, n_far % 2 == 1)
    scores(0, sa_ref)

    def far_pair(pj, carry):
        scores(2 * pj + 1, sb_ref)
        step(2 * pj, sa_ref, None, far_bias)
        scores(2 * pj + 2, sa_ref)
        step(2 * pj + 1, sb_ref, None, far_bias)
        return carry

    lax.fori_loop(0, jnp.maximum(n_far, 0) // 2, far_pair, 0)

    @pl.when(odd_far)
    def _():
        scores(i - 1, sb_ref)
        step(i - 2, sa_ref, None, far_bias)
        scores(i, sa_ref)
        step(i - 1, sb_ref, 1, None)
        step(i, sa_ref, 0, None)

    @pl.when(jnp.logical_and(i >= 1, jnp.logical_not(odd_far)))
    def _():
        scores(i, sb_ref)
        step(i - 1, sa_ref, 1, None)
        step(i, sb_ref, 0, None)

    @pl.when(i == 0)
    def _():
        step(0, sa_ref, 0, None)

    lmb = lam_ref[...]
    lam = (jnp.exp(jnp.sum(lmb[0:1] * lmb[1:2], axis=1, keepdims=True))
           - jnp.exp(jnp.sum(lmb[2:3] * lmb[3:4], axis=1, keepdims=True)) + lam_init)
    o = (acc_ref[0] / l_ref[0] - lam * (acc_ref[1] / l_ref[1])).T
    o = o * lax.rsqrt(jnp.mean(o * o, axis=-1, keepdims=True) + LN_EPS) * ng_ref[...]
    o_ref[...] = (o * (1.0 - lam_init)).astype(o_ref.dtype)


def _diff_attention(h, bias_tiles, rel_bias, diff_lambda, norm_g, *, bsz, seq, t, layer_idx):
    n = bsz * seq
    nq = seq // t
    lam_init = 0.8 - 0.6 * math.exp(-0.3 * layer_idx)
    kcol = GROUP_WIDTH // LANES
    return pl.pallas_call(
        functools.partial(_attn_kernel, t=t, lam_init=lam_init),
        grid=(bsz, A_HEADS, nq),
        in_specs=[
            pl.BlockSpec((t, LANES), lambda b, hh, i: (b * nq + i, hh)),
            pl.BlockSpec((seq, LANES), lambda b, hh, i: (b, kcol + hh)),
            pl.BlockSpec((seq, LANES), lambda b, hh, i: (b, 2 * kcol + hh)),
            pl.BlockSpec((1, 2, t, t), lambda b, hh, i: (hh, 0, 0, 0)),
            pl.BlockSpec(memory_space=pltpu.SMEM),
            pl.BlockSpec((4, A_HEAD_DIM), lambda b, hh, i: (0, 0)),
            pl.BlockSpec((1, LANES), lambda b, hh, i: (0, 0)),
        ],
        out_specs=pl.BlockSpec((t, LANES), lambda b, hh, i: (b * nq + i, hh)),
        out_shape=jax.ShapeDtypeStruct((n, GROUP_WIDTH), BF16),
        scratch_shapes=[
            pltpu.VMEM((nq, LANES, t), BF16),
            pltpu.VMEM((LANES, 2 * t), BF16), pltpu.VMEM((t, 2 * t), F32), pltpu.VMEM((t, 2 * t), F32),
            pltpu.VMEM((2, 1, t), F32), pltpu.VMEM((2, 1, t), F32), pltpu.VMEM((2, LANES, t), F32),
        ],
        compiler_params=_params("parallel", "parallel", "arbitrary"),
        name="diff_attention",
    )(h, h, h, bias_tiles, rel_bias, diff_lambda, norm_g)


def _rope_kernel(pos_ref, inv_ref, cos_ref, sin_ref):
    ang = pos_ref[...].astype(F32) * inv_ref[...]
    lane = lax.broadcasted_iota(jnp.int32, ang.shape, 1)
    cos_ref[...] = jnp.cos(ang)
    sn = jnp.sin(ang)
    sin_ref[...] = jnp.where(lane < B_HEAD_DIM // 2, -sn, sn)


def _rope_tables(pos_b, inv_b, *, tr=1024):
    n = pos_b.shape[0]
    return pl.pallas_call(
        _rope_kernel,
        grid=(n // tr,),
        in_specs=[pl.BlockSpec((tr, LANES), lambda i: (i, 0)), pl.BlockSpec((1, LANES), lambda i: (0, 0))],
        out_specs=[pl.BlockSpec((tr, LANES), lambda i: (i, 0))] * 2,
        out_shape=[jax.ShapeDtypeStruct((n, LANES), F32)] * 2,
        compiler_params=_params("parallel"),
        name="rope_tables",
    )(pos_b, inv_b)


def _retention_consts():
    log_g = np.log(1.0 - 2.0 ** (-5.0 - np.arange(B_HEADS, dtype=np.float64)))
    j = np.arange(B_CHUNK, dtype=np.float64)
    diff = j[:, None] - j[None, :]
    decay_mask = np.where(diff >= 0, np.exp(log_g[:, None, None] * np.maximum(diff, 0.0)), 0.0)
    q_dec = np.exp(log_g[:, None] * (j[None, :] + 1.0))
    k_dec = np.exp(log_g[:, None] * (B_CHUNK - 1.0 - j[None, :]))
    chunk_dec = np.exp(log_g * B_CHUNK)
    bc = lambda a: np.ascontiguousarray(np.broadcast_to(a[:, :, None], (B_HEADS, B_CHUNK, LANES)))
    return (decay_mask.astype(np.float32), bc(q_dec).astype(np.float32), bc(k_dec).astype(np.float32),
            [float(c) for c in chunk_dec])


def _retention_kernel(q_ref, k_ref, v_ref, g_ref, cos_ref, sin_ref, dm_ref, qd_ref, kd_ref, o_ref, st_ref,
                      *, chunk_dec):
    @pl.when(pl.program_id(1) == 0)
    def _():
        st_ref[...] = jnp.zeros_like(st_ref)

    tr = q_ref.shape[0]
    half = B_HEAD_DIM // 2
    for c in range(tr // B_CHUNK):
        rows = slice(c * B_CHUNK, (c + 1) * B_CHUNK)
        cos = cos_ref[rows, :]
        sin = sin_ref[rows, :]
        for hh in range(B_HEADS):
            cols = slice(hh * B_HEAD_DIM, (hh + 1) * B_HEAD_DIM)
            q = q_ref[rows, cols].astype(F32)
            k = k_ref[rows, cols].astype(F32)
            q = q * cos + pltpu.roll(q, half, 1) * sin
            k = (k * cos + pltpu.roll(k, half, 1) * sin) * (B_HEAD_DIM ** -0.5)
            qb = q.astype(BF16)
            vb = v_ref[rows, cols]
            scores = _dot_nt(qb, k.astype(BF16)) * dm_ref[hh]
            st = st_ref[hh]
            o = _dot(scores.astype(BF16), vb) + _dot(qb, st.astype(BF16)) * qd_ref[hh]
            st_ref[hh] = st * chunk_dec[hh] + _dot((k * kd_ref[hh]).T.astype(BF16), vb)
            mu = jnp.mean(o, axis=-1, keepdims=True)
            d = o - mu
            var = jnp.mean(d * d, axis=-1, keepdims=True)
            gate = g_ref[rows, cols].astype(F32)
            o_ref[rows, cols] = (d * lax.rsqrt(var + LN_EPS) * (gate * _sigmoid(gate))).astype(o_ref.dtype)


def _retention(h, cos_t, sin_t, *, bsz, seq, tr=512):
    n = bsz * seq
    ns = seq // tr
    dm, qd, kd, chunk_dec = _retention_consts()
    row = lambda b, s: b * ns + s
    const3 = pl.BlockSpec((B_HEADS, B_CHUNK, LANES), lambda b, s: (0, 0, 0))
    return pl.pallas_call(
        functools.partial(_retention_kernel, chunk_dec=chunk_dec),
        grid=(bsz, ns),
        in_specs=[
            pl.BlockSpec((tr, GROUP_WIDTH), lambda b, s: (row(b, s), 3)),
            pl.BlockSpec((tr, GROUP_WIDTH), lambda b, s: (row(b, s), 4)),
            pl.BlockSpec((tr, GROUP_WIDTH), lambda b, s: (row(b, s), 5)),
            pl.BlockSpec((tr, GROUP_WIDTH), lambda b, s: (row(b, s), 6)),
            pl.BlockSpec((tr, LANES), lambda b, s: (row(b, s), 0)),
            pl.BlockSpec((tr, LANES), lambda b, s: (row(b, s), 0)),
            const3, const3, const3,
        ],
        out_specs=pl.BlockSpec((tr, GROUP_WIDTH), lambda b, s: (row(b, s), 0)),
        out_shape=jax.ShapeDtypeStruct((n, GROUP_WIDTH), BF16),
        scratch_shapes=[pltpu.VMEM((B_HEADS, B_HEAD_DIM, B_HEAD_DIM), F32)],
        compiler_params=_params("parallel", "arbitrary"),
        name="retention",
    )(h, h, h, h, cos_t, sin_t, jnp.asarray(dm), jnp.asarray(qd), jnp.asarray(kd))


def _gelu(x):
    return 0.5 * x * (1.0 + lax.erf(x * (0.5 ** 0.5)))


def _gmlp_kernel(u_ref, v_ref, g_ref, b_ref, ws_ref, bs_ref, o_ref):
    tr = u_ref.shape[0]
    v = _layer_norm(_gelu(v_ref[...].astype(F32)), g_ref[...], b_ref[...]).astype(BF16)
    ti = lax.broadcasted_iota(jnp.int32, (C_CHUNK, C_CHUNK), 0)
    si = lax.broadcasted_iota(jnp.int32, (C_CHUNK, C_CHUNK), 1)
    for gi in range(C_GROUPS):
        cols = slice(gi * LANES, (gi + 1) * LANES)
        w = jnp.where(ti >= si, ws_ref[gi], 0.0).astype(BF16)
        for c in range(tr // C_CHUNK):
            rows = slice(c * C_CHUNK, (c + 1) * C_CHUNK)
            mixed = _dot(w, v[rows, cols]) + bs_ref[gi]
            o_ref[rows, cols] = (_gelu(u_ref[rows, cols].astype(F32)) * mixed).astype(o_ref.dtype)


def _spatial_gating(h, ln_g, ln_b, w_s, bs_b, *, tr=512):
    n = h.shape[0]
    const3 = pl.BlockSpec((C_GROUPS, C_CHUNK, LANES), lambda i: (0, 0, 0))
    vec = pl.BlockSpec((1, GROUP_WIDTH), lambda i: (0, 0))
    return pl.pallas_call(
        _gmlp_kernel,
        grid=(n // tr,),
        in_specs=[
            pl.BlockSpec((tr, GROUP_WIDTH), lambda i: (i, 7)),
            pl.BlockSpec((tr, GROUP_WIDTH), lambda i: (i, 8)),
            vec, vec, const3, const3,
        ],
        out_specs=pl.BlockSpec((tr, GROUP_WIDTH), lambda i: (i, 0)),
        out_shape=jax.ShapeDtypeStruct((n, GROUP_WIDTH), BF16),
        compiler_params=_params("parallel"),
        name="spatial_gating",
    )(h, h, ln_g, ln_b, w_s, bs_b)


def _hgrn_consts():
    c = D_CHUNK
    t = np.arange(c)[:, None]
    r = np.arange(c)[None, :]
    ranges = [r <= t, r > t]
    masks = [r == t]
    m = c // 2
    while m >= 1:
        ref = (t // (2 * m)) * 2 * m + m - 1
        second = (t % (2 * m)) >= m
        ranges.append(np.where(second, (r > ref) & (r <= t), (r > t) & (r <= ref)))
        masks.append(((t // (2 * m)) == (r // (2 * m))) & second & ((r % (2 * m)) < m))
        m //= 2
    return (np.concatenate(ranges, axis=0).astype(np.float32), np.stack(masks).astype(np.float32))


def _hgrn_kernel(q_ref, z_ref, i_ref, g_ref, lbl_ref, ng_ref, rs_ref, pm_ref, o_ref, st_ref, oc_ref, *, layer_idx):
    @pl.when(pl.program_id(1) == 0)
    def _():
        st_ref[...] = jnp.zeros_like(st_ref)

    tr = q_ref.shape[0]
    cs = D_CHUNK

    logits = lbl_ref[...]
    e = jnp.exp(logits - jnp.max(logits, axis=0, keepdims=True))
    soft = e / jnp.sum(e, axis=0, keepdims=True)
    lb = jnp.sum(soft[0:layer_idx + 1], axis=0, keepdims=True) - soft[0:1]
    lb = jnp.maximum(lb, LB_FLOOR)

    for c in range(tr // cs):
        rows = slice(c * cs, (c + 1) * cs)
        z = z_ref[rows, :].astype(F32)
        ez = jnp.exp(-jnp.abs(z))
        r = 1.0 / (1.0 + ez)
        sig_pos = jnp.where(z >= 0, r, ez * r)
        sig_neg = jnp.where(z >= 0, ez * r, r)
        log2_f = jnp.log(sig_pos + lb * sig_neg) * LOG2E
        key_all = (1.0 - lb) * sig_neg
        hi = log2_f.astype(BF16)
        lo = (log2_f - hi.astype(F32)).astype(BF16)
        e2 = _dot(rs_ref[...], jnp.concatenate([hi, lo], axis=1))
        e_all = e2[:, 0:GROUP_WIDTH] + e2[:, GROUP_WIDTH:2 * GROUP_WIDTH]
        for hh in range(D_HEADS):
            cols = slice(hh * LANES, (hh + 1) * LANES)
            q = q_ref[rows, cols].astype(F32)
            k = key_all[:, cols]
            vb = i_ref[rows, cols]
            b = e_all[0:cs, cols]
            rem = e_all[cs:2 * cs, cols]
            st = st_ref[hh]
            o = _dot_nt((q * jnp.exp2(b)).astype(BF16), st.astype(BF16))
            a = jnp.where(pm_ref[0] > 0, _dot_nt(q.astype(BF16), k.astype(BF16)), 0.0)
            for lv in range(D_LEVELS):
                w = jnp.exp2(e_all[(2 + lv) * cs:(3 + lv) * cs, cols])
                a = a + jnp.where(pm_ref[1 + lv] > 0, _dot_nt((q * w).astype(BF16), (k * w).astype(BF16)), 0.0)
            oc_ref[rows, cols] = o + _dot(a.astype(BF16), vb)
            kdec = (k * jnp.exp2(rem)).astype(BF16)
            st_ref[hh] = st * jnp.exp2(b[cs - 1:cs, :]) + _dot(vb.astype(F32).T.astype(BF16), kdec)

    o = oc_ref[...]
    o = o * lax.rsqrt(jnp.mean(o * o, axis=-1, keepdims=True) + LN_EPS) * ng_ref[...]
    gate = g_ref[...].astype(F32)
    o_ref[...] = (o * (gate * _sigmoid(gate))).astype(o_ref.dtype)


def _hgrn2(h, lb_logits, norm_g, *, bsz, seq, layer_idx, tr=256):
    n = bsz * seq
    ns = seq // tr
    row = lambda b, s: b * ns + s
    range_sums, pair_masks = _hgrn_consts()
    return pl.pallas_call(
        functools.partial(_hgrn_kernel, layer_idx=layer_idx),
        grid=(bsz, ns),
        in_specs=[
            pl.BlockSpec((tr, GROUP_WIDTH), lambda b, s: (row(b, s), 9)),
            pl.BlockSpec((tr, GROUP_WIDTH), lambda b, s: (row(b, s), 10)),
            pl.BlockSpec((tr, GROUP_WIDTH), lambda b, s: (row(b, s), 11)),
            pl.BlockSpec((tr, GROUP_WIDTH), lambda b, s: (row(b, s), 12)),
            pl.BlockSpec((DEPTH, GROUP_WIDTH), lambda b, s: (0, 0)),
            pl.BlockSpec((1, GROUP_WIDTH), lambda b, s: (0, 0)),
            pl.BlockSpec(range_sums.shape, lambda b, s: (0, 0)),
            pl.BlockSpec(pair_masks.shape, lambda b, s: (0, 0, 0)),
        ],
        out_specs=pl.BlockSpec((tr, GROUP_WIDTH), lambda b, s: (row(b, s), 0)),
        out_shape=jax.ShapeDtypeStruct((n, GROUP_WIDTH), BF16),
        scratch_shapes=[pltpu.VMEM((D_HEADS, LANES, LANES), F32), pltpu.VMEM((tr, GROUP_WIDTH), F32)],
        compiler_params=_params("parallel", "arbitrary"),
        name="hgrn2",
    )(h, h, h, h, lb_logits, norm_g, jnp.asarray(range_sums, BF16), jnp.asarray(pair_masks))


def _out_proj_kernel(x_ref, a_ref, b_ref, c_ref, d_ref, w_ref, g_ref, bb_ref, o_ref):
    acc = _dot(a_ref[...], w_ref[0])
    acc += _dot(b_ref[...], w_ref[1])
    acc += _dot(c_ref[...], w_ref[2])
    acc += _dot(d_ref[...], w_ref[3])
    o_ref[...] = _layer_norm(ALPHA * x_ref[...] + acc, g_ref[...], bb_ref[...])


def _out_proj_ln(x, parts, w, g, b, *, tm=512):
    n, d = x.shape
    part = pl.BlockSpec((tm, GROUP_WIDTH), lambda i: (i, 0))
    vec = pl.BlockSpec((1, d), lambda i: (0, 0))
    return pl.pallas_call(
        _out_proj_kernel,
        grid=(n // tm,),
        in_specs=[pl.BlockSpec((tm, d), lambda i: (i, 0)), part, part, part, part,
                  pl.BlockSpec((4, GROUP_WIDTH, d), lambda i: (0, 0, 0)), vec, vec],
        out_specs=pl.BlockSpec((tm, d), lambda i: (i, 0)),
        out_shape=jax.ShapeDtypeStruct((n, d), F32),
        compiler_params=_params("parallel"),
        name="out_proj_ln",
    )(x, *parts, w, g, b)


def _ple_kernel(x_ref, p_ref, wg_ref, we_ref, g_ref, b_ref, o_ref):
    x = x_ref[...]
    gate = _sigmoid(_dot(x.astype(BF16), wg_ref[...]))
    emb = _dot(p_ref[...].astype(BF16), we_ref[...])
    o_ref[...] = _layer_norm(ALPHA * x + gate * emb, g_ref[...], b_ref[...])


def _ple_ln(x, p, wg, we, g, b, *, tm=512):
    n, d = x.shape
    vec = pl.BlockSpec((1, d), lambda i: (0, 0))
    return pl.pallas_call(
        _ple_kernel,
        grid=(n // tm,),
        in_specs=[pl.BlockSpec((tm, d), lambda i: (i, 0)), pl.BlockSpec((tm, PLE_DIM), lambda i: (i, 0)),
                  pl.BlockSpec((d, d), lambda i: (0, 0)), pl.BlockSpec((PLE_DIM, d), lambda i: (0, 0)), vec, vec],
        out_specs=pl.BlockSpec((tm, d), lambda i: (i, 0)),
        out_shape=jax.ShapeDtypeStruct((n, d), F32),
        compiler_params=_params("parallel"),
        name="ple_ln",
    )(x, p, wg, we, g, b)


def _rotary_column_order():
    per_head = np.concatenate([np.arange(0, B_HEAD_DIM, 2), np.arange(1, B_HEAD_DIM, 2)])
    return np.concatenate([hh * B_HEAD_DIM + per_head for hh in range(B_HEADS)])


def _mix_in_weight(w):
    perm = _rotary_column_order()
    cols = np.arange(IN_COLS)
    for blk in (3, 4):
        cols[blk * GROUP_WIDTH:(blk + 1) * GROUP_WIDTH] = blk * GROUP_WIDTH + perm
    return w[:, cols].astype(BF16)


def kernel(x, p, positions, ffn1_w_in, ffn1_w_out, w_mix_in, w_mix_out, rel_bias, diff_lambda, diff_norm_g,
           gmlp_ln_g, gmlp_ln_b, gmlp_w_s, gmlp_b_s, hgrn_lb_logits, hgrn_norm_g, ffn2_w_in, ffn2_w_out,
           ple_w_gate, ple_w_proj, ln_g, ln_b):
    bsz, seq, d = x.shape
    n = bsz * seq
    t_attn = min(512, seq)
    xs = x.reshape(n, d)

    pos_b = jnp.broadcast_to(positions.reshape(n, 1), (n, LANES))
    inv = ROPE_BASE ** (-jnp.linspace(0.0, 1.0, B_HEAD_DIM // 2, dtype=F32))
    inv_b = jnp.concatenate([inv, inv]).reshape(1, LANES)
    cos_t, sin_t = _rope_tables(pos_b, inv_b, tr=min(1024, n))
    bias_tiles = _bias_tiles(rel_bias, t_attn)

    for i in range(DEPTH):
        lng = ln_g[i].reshape(4, 1, d)
        lnb = ln_b[i].reshape(4, 1, d)
        xs = _ffn_ln(xs, ffn1_w_in[i].astype(BF16), ffn1_w_out[i].astype(BF16), lng[0], lnb[0])
        h = _in_proj(xs, _mix_in_weight(w_mix_in[i]))
        out_a = _diff_attention(h, bias_tiles, rel_bias, diff_lambda[i], diff_norm_g[i].reshape(1, LANES),
                                bsz=bsz, seq=seq, t=t_attn, layer_idx=i)
        out_b = _retention(h, cos_t, sin_t, bsz=bsz, seq=seq)
        bs_b = jnp.broadcast_to(gmlp_b_s[i][:, :, None], (C_GROUPS, C_CHUNK, LANES))
        out_c = _spatial_gating(h, gmlp_ln_g[i].reshape(1, GROUP_WIDTH), gmlp_ln_b[i].reshape(1, GROUP_WIDTH),
                                gmlp_w_s[i], bs_b)
        out_d = _hgrn2(h, hgrn_lb_logits, hgrn_norm_g[i].reshape(1, GROUP_WIDTH), bsz=bsz, seq=seq, layer_idx=i)
        xs = _out_proj_ln(xs, (out_a, out_b, out_c, out_d),
                          w_mix_out[i].astype(BF16).reshape(4, GROUP_WIDTH, d), lng[1], lnb[1])
        xs = _ffn_ln(xs, ffn2_w_in[i].astype(BF16), ffn2_w_out[i].astype(BF16), lng[2], lnb[2])
        xs = _ple_ln(xs, p[i].reshape(n, PLE_DIM), ple_w_gate[i].astype(BF16), ple_w_proj[i].astype(BF16),
                     lng[3], lnb[3])
    return xs.reshape(bsz, seq, d)
```

```python
import functools
import math

import numpy as np
import jax
import jax.numpy as jnp
from jax import lax
from jax.experimental import pallas as pl
from jax.experimental.pallas import tpu as pltpu

D_MODEL = 2048
DEPTH = 2
PLE_DIM = 256
D_FF = 5632
GROUP_WIDTH = 512

A_HEADS = 4
A_HEAD_DIM = 64
N_BUCKETS = 32
MAX_DISTANCE = 128

B_HEADS = 4
B_HEAD_DIM = 128
B_CHUNK = 128
ROPE_BASE = 10000.0

C_GROUPS = 4
C_CHUNK = 128

D_HEADS = 4
D_CHUNK = 64
D_LEVELS = 6

IN_COLS = 13 * GROUP_WIDTH
ALPHA = (2 * DEPTH) ** 0.25
LN_EPS = 1e-5
MASK_VALUE = -1e30
LB_FLOOR = 1e-30
LOG2E = math.log2(math.e)

LANES = 128
VMEM_LIMIT = 56 * 1024 * 1024
FFN_VMEM_LIMIT = 63 * 1024 * 1024

F32 = jnp.float32
BF16 = jnp.bfloat16


def _dot(a, b):
    return jnp.dot(a, b, preferred_element_type=F32)


def _dot_nt(a, b):
    return lax.dot_general(a, b, (((1,), (1,)), ((), ())), preferred_element_type=F32)


def _sigmoid(x):
    return 1.0 / (1.0 + jnp.exp(-x))


def _layer_norm(y, g, b):
    mu = jnp.mean(y, axis=-1, keepdims=True)
    d = y - mu
    var = jnp.mean(d * d, axis=-1, keepdims=True)
    return d * lax.rsqrt(var + LN_EPS) * g + b


def _params(*sem, vmem=VMEM_LIMIT):
    return pltpu.CompilerParams(dimension_semantics=sem, vmem_limit_bytes=vmem)


def _ffn_kernel(x_ref, wg_ref, wu_ref, wo_ref, g_ref, b_ref, o_ref, xb_ref):
    j = pl.program_id(1)

    @pl.when(j == 0)
    def _():
        xb_ref[...] = x_ref[...].astype(BF16)
        o_ref[...] = jnp.zeros_like(o_ref)

    xb = xb_ref[...]
    hg = _dot(xb, wg_ref[...])
    hu = _dot(xb, wu_ref[...])
    a = (hg * _sigmoid(hg) * hu).astype(BF16)
    o_ref[...] += _dot(a, wo_ref[...])

    @pl.when(j == pl.num_programs(1) - 1)
    def _():
        y = ALPHA * x_ref[...] + 0.5 * o_ref[...]
        o_ref[...] = _layer_norm(y, g_ref[...], b_ref[...])


def _ffn_ln(x, w_in, w_out, g, b, *, tm=1024, tf=512):
    n, d = x.shape
    f = w_out.shape[0]
    nf = f // tf
    return pl.pallas_call(
        _ffn_kernel,
        grid=(n // tm, nf),
        in_specs=[
            pl.BlockSpec((tm, d), lambda i, j: (i, 0)),
            pl.BlockSpec((d, tf), lambda i, j: (0, j)),
            pl.BlockSpec((d, tf), lambda i, j: (0, j + nf)),
            pl.BlockSpec((tf, d), lambda i, j: (j, 0)),
            pl.BlockSpec((1, d), lambda i, j: (0, 0)),
            pl.BlockSpec((1, d), lambda i, j: (0, 0)),
        ],
        out_specs=pl.BlockSpec((tm, d), lambda i, j: (i, 0)),
        out_shape=jax.ShapeDtypeStruct((n, d), F32),
        scratch_shapes=[pltpu.VMEM((tm, d), BF16)],
        compiler_params=_params("parallel", "arbitrary", vmem=FFN_VMEM_LIMIT),
        name="ffn_ln",
    )(x, w_in, w_in, w_out, g, b)


def _in_proj_kernel(x_ref, w_ref, o_ref, xb_ref):
    @pl.when(pl.program_id(1) == 0)
    def _():
        xb_ref[...] = x_ref[...].astype(BF16)

    o_ref[...] = _dot(xb_ref[...], w_ref[...])


def _in_proj(x, w, *, tm=2048, tn=512):
    n, d = x.shape
    c = w.shape[1]
    return pl.pallas_call(
        _in_proj_kernel,
        grid=(n // tm, c // tn),
        in_specs=[
            pl.BlockSpec((tm, d), lambda i, j: (i, 0)),
            pl.BlockSpec((d, tn), lambda i, j: (0, j)),
        ],
        out_specs=pl.BlockSpec((tm, tn), lambda i, j: (i, j)),
        out_shape=jax.ShapeDtypeStruct((n, c), F32),
        scratch_shapes=[pltpu.VMEM((tm, d), BF16)],
        compiler_params=_params("parallel", "arbitrary"),
        name="in_proj",
    )(x, w)


def _bias_tiles_kernel(rb_ref, o_ref, *, t):
    h = pl.program_id(0)
    j = lax.broadcasted_iota(jnp.int32, (t, t), 0)
    i = lax.broadcasted_iota(jnp.int32, (t, t), 1)
    max_exact = N_BUCKETS // 2
    for sel in range(2):
        n = jnp.maximum(i - j + sel * t, 0)
        nf = jnp.maximum(n, 1).astype(F32)
        large = max_exact + (jnp.log(nf / max_exact) / math.log(MAX_DISTANCE / max_exact)
                             * (N_BUCKETS - max_exact)).astype(jnp.int32)
        large = jnp.minimum(large, N_BUCKETS - 1)
        bucket = jnp.where(n < max_exact, n, large)
        val = jnp.zeros((t, t), F32)
        for bkt in range(N_BUCKETS):
            val = jnp.where(bucket == bkt, rb_ref[bkt, h] * LOG2E, val)
        if sel == 0:
            val = jnp.where(i >= j, val, MASK_VALUE)
        o_ref[0, sel] = val


def _bias_tiles(rel_bias, t):
    return pl.pallas_call(
        functools.partial(_bias_tiles_kernel, t=t),
        grid=(A_HEADS,),
        in_specs=[pl.BlockSpec(memory_space=pltpu.SMEM)],
        out_specs=pl.BlockSpec((1, 2, t, t), lambda h: (h, 0, 0, 0)),
        out_shape=jax.ShapeDtypeStruct((A_HEADS, 2, t, t), F32),
        compiler_params=_params("arbitrary"),
        name="t5_bias_tiles",
    )(rel_bias)


def _attn_kernel(q_ref, k_ref, v_ref, bt_ref, rb_ref, lam_ref, ng_ref, o_ref,
                 kb_ref, vt_ref, qst_ref, sa_ref, sb_ref, m_ref, l_ref, acc_ref, *, t, lam_init):
    h = pl.program_id(1)
    i = pl.program_id(2)

    @pl.when(i == 0)
    def _():
        def fill(c, carry):
            rows = pl.ds(pl.multiple_of(c * t, t), t)
            kb_ref[c] = k_ref[rows, :].astype(BF16)
            vt_ref[c] = v_ref[rows, :].T.astype(BF16)
            return carry

        lax.fori_loop(0, kb_ref.shape[0], fill, 0)

    qt = (q_ref[...] * (A_HEAD_DIM ** -0.5 * LOG2E)).T
    feat = lax.broadcasted_iota(jnp.int32, qt.shape, 0)
    qst_ref[:, 0:t] = jnp.where(feat < A_HEAD_DIM, qt, 0.0).astype(BF16)
    qst_ref[:, t:2 * t] = jnp.where(feat >= A_HEAD_DIM, qt, 0.0).astype(BF16)
    m_ref[...] = jnp.full(m_ref.shape, MASK_VALUE, F32)
    l_ref[...] = jnp.zeros_like(l_ref)
    acc_ref[...] = jnp.zeros_like(acc_ref)

    def scores(kj, s2_ref):
        s2_ref[...] = _dot(kb_ref[kj], qst_ref[...])

    def step(kj, s2_ref, tile_sel, const_bias):
        vt = vt_ref[kj]
        for c in range(2):
            s = s2_ref[:, c * t:(c + 1) * t]
            if tile_sel is not None:
                s = s + bt_ref[0, tile_sel]
            m_old = m_ref[c]
            smax = jnp.max(s, axis=0, keepdims=True)
            if const_bias is not None:
                smax = smax + const_bias
            m_new = jnp.maximum(m_old, smax)
            shift = m_new if const_bias is None else m_new - const_bias
            p = jnp.exp2(s - shift)
            corr = jnp.exp2(m_old - m_new)
            l_ref[c] = corr * l_ref[c] + jnp.sum(p, axis=0, keepdims=True)
            acc_ref[c] = corr * acc_ref[c] + _dot(vt, p.astype(BF16))
            m_ref[c] = m_new

    far_bias = rb_ref[N_BUCKETS - 1, h] * LOG2E

    n_far = i - 1
    odd_far = jnp.logical_and(i >= 2, n_far % 2 == 1)
    scores(0, sa_ref)

    def far_pair(pj, carry):
        scores(2 * pj + 1, sb_ref)
        step(2 * pj, sa_ref, None, far_bias)
        scores(2 * pj + 2, sa_ref)
        step(2 * pj + 1, sb_ref, None, far_bias)
        return carry

    lax.fori_loop(0, jnp.maximum(n_far, 0) // 2, far_pair, 0)

    @pl.when(odd_far)
    def _():
        scores(i - 1, sb_ref)
        step(i - 2, sa_ref, None, far_bias)
        scores(i, sa_ref)
        step(i - 1, sb_ref, 1, None)
        step(i, sa_ref, 0, None)

    @pl.when(jnp.logical_and(i >= 1, jnp.logical_not(odd_far)))
    def _():
        scores(i, sb_ref)
        step(i - 1, sa_ref, 1, None)
        step(i, sb_ref, 0, None)

    @pl.when(i == 0)
    def _():
        step(0, sa_ref, 0, None)

    lmb = lam_ref[...]
    lam = (jnp.exp(jnp.sum(lmb[0:1] * lmb[1:2], axis=1, keepdims=True))
           - jnp.exp(jnp.sum(lmb[2:3] * lmb[3:4], axis=1, keepdims=True)) + lam_init)
    o = (acc_ref[0] / l_ref[0] - lam * (acc_ref[1] / l_ref[1])).T
    o = o * lax.rsqrt(jnp.mean(o * o, axis=-1, keepdims=True) + LN_EPS) * ng_ref[...]
    o_ref[...] = (o * (1.0 - lam_init)).astype(o_ref.dtype)


def _diff_attention(h, bias_tiles, rel_bias, diff_lambda, norm_g, *, bsz, seq, t, layer_idx):
    n = bsz * seq
    nq = seq // t
    lam_init = 0.8 - 0.6 * math.exp(-0.3 * layer_idx)
    kcol = GROUP_WIDTH // LANES
    return pl.pallas_call(
        functools.partial(_attn_kernel, t=t, lam_init=lam_init),
        grid=(bsz, A_HEADS, nq),
        in_specs=[
            pl.BlockSpec((t, LANES), lambda b, hh, i: (b * nq + i, hh)),
            pl.BlockSpec((seq, LANES), lambda b, hh, i: (b, kcol + hh)),
            pl.BlockSpec((seq, LANES), lambda b, hh, i: (b, 2 * kcol + hh)),
            pl.BlockSpec((1, 2, t, t), lambda b, hh, i: (hh, 0, 0, 0)),
            pl.BlockSpec(memory_space=pltpu.SMEM),
            pl.BlockSpec((4, A_HEAD_DIM), lambda b, hh, i: (0, 0)),
            pl.BlockSpec((1, LANES), lambda b, hh, i: (0, 0)),
        ],
        out_specs=pl.BlockSpec((t, LANES), lambda b, hh, i: (b * nq + i, hh)),
        out_shape=jax.ShapeDtypeStruct((n, GROUP_WIDTH), BF16),
        scratch_shapes=[
            pltpu.VMEM((nq, t, LANES), BF16), pltpu.VMEM((nq, LANES, t), BF16),
            pltpu.VMEM((LANES, 2 * t), BF16), pltpu.VMEM((t, 2 * t), F32), pltpu.VMEM((t, 2 * t), F32),
            pltpu.VMEM((2, 1, t), F32), pltpu.VMEM((2, 1, t), F32), pltpu.VMEM((2, LANES, t), F32),
        ],
        compiler_params=_params("parallel", "parallel", "arbitrary"),
        name="diff_attention",
    )(h, h, h, bias_tiles, rel_bias, diff_lambda, norm_g)


def _rope_kernel(pos_ref, inv_ref, cos_ref, sin_ref):
    ang = pos_ref[...].astype(F32) * inv_ref[...]
    lane = lax.broadcasted_iota(jnp.int32, ang.shape, 1)
    cos_ref[...] = jnp.cos(ang)
    sn = jnp.sin(ang)
    sin_ref[...] = jnp.where(lane < B_HEAD_DIM // 2, -sn, sn)


def _rope_tables(pos_b, inv_b, *, tr=1024):
    n = pos_b.shape[0]
    return pl.pallas_call(
        _rope_kernel,
        grid=(n // tr,),
        in_specs=[pl.BlockSpec((tr, LANES), lambda i: (i, 0)), pl.BlockSpec((1, LANES), lambda i: (0, 0))],
        out_specs=[pl.BlockSpec((tr, LANES), lambda i: (i, 0))] * 2,
        out_shape=[jax.ShapeDtypeStruct((n, LANES), F32)] * 2,
        compiler_params=_params("parallel"),
        name="rope_tables",
    )(pos_b, inv_b)


def _retention_consts():
    log_g = np.log(1.0 - 2.0 ** (-5.0 - np.arange(B_HEADS, dtype=np.float64)))
    j = np.arange(B_CHUNK, dtype=np.float64)
    diff = j[:, None] - j[None, :]
    decay_mask = np.where(diff >= 0, np.exp(log_g[:, None, None] * np.maximum(diff, 0.0)), 0.0)
    q_dec = np.exp(log_g[:, None] * (j[None, :] + 1.0))
    k_dec = np.exp(log_g[:, None] * (B_CHUNK - 1.0 - j[None, :]))
    chunk_dec = np.exp(log_g * B_CHUNK)
    bc = lambda a: np.ascontiguousarray(np.broadcast_to(a[:, :, None], (B_HEADS, B_CHUNK, LANES)))
    return (decay_mask.astype(np.float32), bc(q_dec).astype(np.float32), bc(k_dec).astype(np.float32),
            [float(c) for c in chunk_dec])


def _retention_kernel(q_ref, k_ref, v_ref, g_ref, cos_ref, sin_ref, dm_ref, qd_ref, kd_ref, o_ref, st_ref,
                      *, chunk_dec):
    @pl.when(pl.program_id(1) == 0)
    def _():
        st_ref[...] = jnp.zeros_like(st_ref)

    tr = q_ref.shape[0]
    half = B_HEAD_DIM // 2
    for c in range(tr // B_CHUNK):
        rows = slice(c * B_CHUNK, (c + 1) * B_CHUNK)
        cos = cos_ref[rows, :]
        sin = sin_ref[rows, :]
        for hh in range(B_HEADS):
            cols = slice(hh * B_HEAD_DIM, (hh + 1) * B_HEAD_DIM)
            q = q_ref[rows, cols]
            k = k_ref[rows, cols]
            q = q * cos + pltpu.roll(q, half, 1) * sin
            k = (k * cos + pltpu.roll(k, half, 1) * sin) * (B_HEAD_DIM ** -0.5)
            qb = q.astype(BF16)
            vb = v_ref[rows, cols].astype(BF16)
            scores = _dot_nt(qb, k.astype(BF16)) * dm_ref[hh]
            st = st_ref[hh]
            o = _dot(scores.astype(BF16), vb) + _dot(qb, st.astype(BF16)) * qd_ref[hh]
            st_ref[hh] = st * chunk_dec[hh] + _dot((k * kd_ref[hh]).T.astype(BF16), vb)
            mu = jnp.mean(o, axis=-1, keepdims=True)
            d = o - mu
            var = jnp.mean(d * d, axis=-1, keepdims=True)
            gate = g_ref[rows, cols]
            o_ref[rows, cols] = (d * lax.rsqrt(var + LN_EPS) * (gate * _sigmoid(gate))).astype(o_ref.dtype)


def _retention(h, cos_t, sin_t, *, bsz, seq, tr=512):
    n = bsz * seq
    ns = seq // tr
    dm, qd, kd, chunk_dec = _retention_consts()
    row = lambda b, s: b * ns + s
    const3 = pl.BlockSpec((B_HEADS, B_CHUNK, LANES), lambda b, s: (0, 0, 0))
    return pl.pallas_call(
        functools.partial(_retention_kernel, chunk_dec=chunk_dec),
        grid=(bsz, ns),
        in_specs=[
            pl.BlockSpec((tr, GROUP_WIDTH), lambda b, s: (row(b, s), 3)),
            pl.BlockSpec((tr, GROUP_WIDTH), lambda b, s: (row(b, s), 4)),
            pl.BlockSpec((tr, GROUP_WIDTH), lambda b, s: (row(b, s), 5)),
            pl.BlockSpec((tr, GROUP_WIDTH), lambda b, s: (row(b, s), 6)),
            pl.BlockSpec((tr, LANES), lambda b, s: (row(b, s), 0)),
            pl.BlockSpec((tr, LANES), lambda b, s: (row(b, s), 0)),
            const3, const3, const3,
        ],
        out_specs=pl.BlockSpec((tr, GROUP_WIDTH), lambda b, s: (row(b, s), 0)),
        out_shape=jax.ShapeDtypeStruct((n, GROUP_WIDTH), BF16),
        scratch_shapes=[pltpu.VMEM((B_HEADS, B_HEAD_DIM, B_HEAD_DIM), F32)],
        compiler_params=_params("parallel", "arbitrary"),
        name="retention",
    )(h, h, h, h, cos_t, sin_t, jnp.asarray(dm), jnp.asarray(qd), jnp.asarray(kd))


def _gelu(x):
    return 0.5 * x * (1.0 + lax.erf(x * (0.5 ** 0.5)))


def _gmlp_kernel(u_ref, v_ref, g_ref, b_ref, ws_ref, bs_ref, o_ref):
    tr = u_ref.shape[0]
    v = _layer_norm(_gelu(v_ref[...]), g_ref[...], b_ref[...]).astype(BF16)
    ti = lax.broadcasted_iota(jnp.int32, (C_CHUNK, C_CHUNK), 0)
    si = lax.broadcasted_iota(jnp.int32, (C_CHUNK, C_CHUNK), 1)
    for gi in range(C_GROUPS):
        cols = slice(gi * LANES, (gi + 1) * LANES)
        w = jnp.where(ti >= si, ws_ref[gi], 0.0).astype(BF16)
        for c in range(tr // C_CHUNK):
            rows = slice(c * C_CHUNK, (c + 1) * C_CHUNK)
            mixed = _dot(w, v[rows, cols]) + bs_ref[gi]
            o_ref[rows, cols] = (_gelu(u_ref[rows, cols]) * mixed).astype(o_ref.dtype)


def _spatial_gating(h, ln_g, ln_b, w_s, bs_b, *, tr=512):
    n = h.shape[0]
    const3 = pl.BlockSpec((C_GROUPS, C_CHUNK, LANES), lambda i: (0, 0, 0))
    vec = pl.BlockSpec((1, GROUP_WIDTH), lambda i: (0, 0))
    return pl.pallas_call(
        _gmlp_kernel,
        grid=(n // tr,),
        in_specs=[
            pl.BlockSpec((tr, GROUP_WIDTH), lambda i: (i, 7)),
            pl.BlockSpec((tr, GROUP_WIDTH), lambda i: (i, 8)),
            vec, vec, const3, const3,
        ],
        out_specs=pl.BlockSpec((tr, GROUP_WIDTH), lambda i: (i, 0)),
        out_shape=jax.ShapeDtypeStruct((n, GROUP_WIDTH), BF16),
        compiler_params=_params("parallel"),
        name="spatial_gating",
    )(h, h, ln_g, ln_b, w_s, bs_b)


def _hgrn_consts():
    c = D_CHUNK
    t = np.arange(c)[:, None]
    r = np.arange(c)[None, :]
    ranges = [r <= t, r > t]
    masks = [r == t]
    m = c // 2
    while m >= 1:
        ref = (t // (2 * m)) * 2 * m + m - 1
        second = (t % (2 * m)) >= m
        ranges.append(np.where(second, (r > ref) & (r <= t), (r > t) & (r <= ref)))
        masks.append(((t // (2 * m)) == (r // (2 * m))) & second & ((r % (2 * m)) < m))
        m //= 2
    ranges = np.concatenate(ranges, axis=0)
    return (np.concatenate([ranges, ranges], axis=1).astype(np.float32), np.stack(masks).astype(np.float32))


def _hgrn_kernel(q_ref, z_ref, i_ref, g_ref, lbl_ref, ng_ref, rs_ref, pm_ref, o_ref,
                 st_ref, oc_ref, key_ref, ex_ref, *, layer_idx):
    @pl.when(pl.program_id(1) == 0)
    def _():
        st_ref[...] = jnp.zeros_like(st_ref)

    tr = q_ref.shape[0]
    cs = D_CHUNK

    logits = lbl_ref[...]
    e = jnp.exp(logits - jnp.max(logits, axis=0, keepdims=True))
    soft = e / jnp.sum(e, axis=0, keepdims=True)
    lb = jnp.sum(soft[0:layer_idx + 1], axis=0, keepdims=True) - soft[0:1]
    lb = jnp.maximum(lb, LB_FLOOR)

    for c in range(tr // cs):
        rows = slice(c * cs, (c + 1) * cs)
        z = z_ref[rows, :]
        ez = jnp.exp(-jnp.abs(z))
        r = 1.0 / (1.0 + ez)
        sig_pos = jnp.where(z >= 0, r, ez * r)
        sig_neg = jnp.where(z >= 0, ez * r, r)
        log2_f = jnp.log(sig_pos + lb * sig_neg) * LOG2E
        key_ref[rows, :] = (1.0 - lb) * sig_neg
        hi = log2_f.astype(BF16)
        lo = (log2_f - hi.astype(F32)).astype(BF16)
        ex_ref[c] = _dot(rs_ref[...], jnp.concatenate([hi, lo], axis=0))

    for c in range(tr // cs):
        rows = slice(c * cs, (c + 1) * cs)
        for hh in range(D_HEADS):
            cols = slice(hh * LANES, (hh + 1) * LANES)
            q = q_ref[rows, cols]
            k = key_ref[rows, cols]
            vb = i_ref[rows, cols].astype(BF16)
            e_all = ex_ref.at[c]
            b = e_all[0:cs, cols]
            rem = e_all[cs:2 * cs, cols]
            st = st_ref[hh]
            o = _dot_nt((q * jnp.exp2(b)).astype(BF16), st.astype(BF16))
            a = jnp.where(pm_ref[0] > 0, _dot_nt(q.astype(BF16), k.astype(BF16)), 0.0)
            for lv in range(D_LEVELS):
                w = jnp.exp2(e_all[(2 + lv) * cs:(3 + lv) * cs, cols])
                a = a + jnp.where(pm_ref[1 + lv] > 0, _dot_nt((q * w).astype(BF16), (k * w).astype(BF16)), 0.0)
            oc_ref[rows, cols] = o + _dot(a.astype(BF16), vb)
            kdec = (k * jnp.exp2(rem)).astype(BF16)
            st_ref[hh] = st * jnp.exp2(b[cs - 1:cs, :]) + _dot(i_ref[rows, cols].T.astype(BF16), kdec)

    o = oc_ref[...]
    o = o * lax.rsqrt(jnp.mean(o * o, axis=-1, keepdims=True) + LN_EPS) * ng_ref[...]
    gate = g_ref[...]
    o_ref[...] = (o * (gate * _sigmoid(gate))).astype(o_ref.dtype)


def _hgrn2(h, lb_logits, norm_g, *, bsz, seq, layer_idx, tr=256):
    n = bsz * seq
    ns = seq // tr
    row = lambda b, s: b * ns + s
    range_sums, pair_masks = _hgrn_consts()
    return pl.pallas_call(
        functools.partial(_hgrn_kernel, layer_idx=layer_idx),
        grid=(bsz, ns),
        in_specs=[
            pl.BlockSpec((tr, GROUP_WIDTH), lambda b, s: (row(b, s), 9)),
            pl.BlockSpec((tr, GROUP_WIDTH), lambda b, s: (row(b, s), 10)),
            pl.BlockSpec((tr, GROUP_WIDTH), lambda b, s: (row(b, s), 11)),
            pl.BlockSpec((tr, GROUP_WIDTH), lambda b, s: (row(b, s), 12)),
            pl.BlockSpec((DEPTH, GROUP_WIDTH), lambda b, s: (0, 0)),
            pl.BlockSpec((1, GROUP_WIDTH), lambda b, s: (0, 0)),
            pl.BlockSpec(range_sums.shape, lambda b, s: (0, 0)),
            pl.BlockSpec(pair_masks.shape, lambda b, s: (0, 0, 0)),
        ],
        out_specs=pl.BlockSpec((tr, GROUP_WIDTH), lambda b, s: (row(b, s), 0)),
        out_shape=jax.ShapeDtypeStruct((n, GROUP_WIDTH), BF16),
        scratch_shapes=[pltpu.VMEM((D_HEADS, LANES, LANES), F32), pltpu.VMEM((tr, GROUP_WIDTH), F32),
                        pltpu.VMEM((tr, GROUP_WIDTH), F32),
                        pltpu.VMEM((tr // D_CHUNK, range_sums.shape[0], GROUP_WIDTH), F32)],
        compiler_params=_params("parallel", "arbitrary"),
        name="hgrn2",
    )(h, h, h, h, lb_logits, norm_g, jnp.asarray(range_sums, BF16), jnp.asarray(pair_masks))


def _out_proj_kernel(x_ref, a_ref, b_ref, c_ref, d_ref, w_ref, g_ref, bb_ref, o_ref):
    acc = _dot(a_ref[...], w_ref[0])
    acc += _dot(b_ref[...], w_ref[1])
    acc += _dot(c_ref[...], w_ref[2])
    acc += _dot(d_ref[...], w_ref[3])
    o_ref[...] = _layer_norm(ALPHA * x_ref[...] + acc, g_ref[...], bb_ref[...])


def _out_proj_ln(x, parts, w, g, b, *, tm=512):
    n, d = x.shape
    part = pl.BlockSpec((tm, GROUP_WIDTH), lambda i: (i, 0))
    vec = pl.BlockSpec((1, d), lambda i: (0, 0))
    return pl.pallas_call(
        _out_proj_kernel,
        grid=(n // tm,),
        in_specs=[pl.BlockSpec((tm, d), lambda i: (i, 0)), part, part, part, part,
                  pl.BlockSpec((4, GROUP_WIDTH, d), lambda i: (0, 0, 0)), vec, vec],
        out_specs=pl.BlockSpec((tm, d), lambda i: (i, 0)),
        out_shape=jax.ShapeDtypeStruct((n, d), F32),
        compiler_params=_params("parallel"),
        name="out_proj_ln",
    )(x, *parts, w, g, b)


def _ple_kernel(x_ref, p_ref, wg_ref, we_ref, g_ref, b_ref, o_ref):
    x = x_ref[...]
    gate = _sigmoid(_dot(x.astype(BF16), wg_ref[...]))
    emb = _dot(p_ref[...].astype(BF16), we_ref[...])
    o_ref[...] = _layer_norm(ALPHA * x + gate * emb, g_ref[...], b_ref[...])


def _ple_ln(x, p_all, layer_idx, wg, we, g, b, *, tm=512):
    n, d = x.shape
    nt = n // tm
    vec = pl.BlockSpec((1, d), lambda i: (0, 0))
    return pl.pallas_call(
        _ple_kernel,
        grid=(nt,),
        in_specs=[pl.BlockSpec((tm, d), lambda i: (i, 0)),
                  pl.BlockSpec((tm, PLE_DIM), lambda i: (layer_idx * nt + i, 0)),
                  pl.BlockSpec((d, d), lambda i: (0, 0)), pl.BlockSpec((PLE_DIM, d), lambda i: (0, 0)), vec, vec],
        out_specs=pl.BlockSpec((tm, d), lambda i: (i, 0)),
        out_shape=jax.ShapeDtypeStruct((n, d), F32),
        compiler_params=_params("parallel"),
        name="ple_ln",
    )(x, p_all, wg, we, g, b)


def _rotary_column_order():
    per_head = np.concatenate([np.arange(0, B_HEAD_DIM, 2), np.arange(1, B_HEAD_DIM, 2)])
    return np.concatenate([hh * B_HEAD_DIM + per_head for hh in range(B_HEADS)])


def _mix_in_weight(w):
    perm = _rotary_column_order()
    cols = np.arange(IN_COLS)
    for blk in (3, 4):
        cols[blk * GROUP_WIDTH:(blk + 1) * GROUP_WIDTH] = blk * GROUP_WIDTH + perm
    return w[:, cols].astype(BF16)


def kernel(x, p, positions, ffn1_w_in, ffn1_w_out, w_mix_in, w_mix_out, rel_bias, diff_lambda, diff_norm_g,
           gmlp_ln_g, gmlp_ln_b, gmlp_w_s, gmlp_b_s, hgrn_lb_logits, hgrn_norm_g, ffn2_w_in, ffn2_w_out,
           ple_w_gate, ple_w_proj, ln_g, ln_b):
    bsz, seq, d = x.shape
    n = bsz * seq
    t_attn = min(512, seq)
    xs = x.reshape(n, d)

    pos_b = jnp.broadcast_to(positions.reshape(n, 1), (n, LANES))
    inv = ROPE_BASE ** (-jnp.linspace(0.0, 1.0, B_HEAD_DIM // 2, dtype=F32))
    inv_b = jnp.concatenate([inv, inv]).reshape(1, LANES)
    cos_t, sin_t = _rope_tables(pos_b, inv_b, tr=min(1024, n))
    bias_tiles = _bias_tiles(rel_bias, t_attn)

    for i in range(DEPTH):
        lng = ln_g[i].reshape(4, 1, d)
        lnb = ln_b[i].reshape(4, 1, d)
        xs = _ffn_ln(xs, ffn1_w_in[i].astype(BF16), ffn1_w_out[i].astype(BF16), lng[0], lnb[0])
        h = _in_proj(xs, _mix_in_weight(w_mix_in[i]))
        out_a = _diff_attention(h, bias_tiles, rel_bias, diff_lambda[i], diff_norm_g[i].reshape(1, LANES),
                                bsz=bsz, seq=seq, t=t_attn, layer_idx=i)
        out_b = _retention(h, cos_t, sin_t, bsz=bsz, seq=seq)
        bs_b = jnp.broadcast_to(gmlp_b_s[i][:, :, None], (C_GROUPS, C_CHUNK, LANES))
        out_c = _spatial_gating(h, gmlp_ln_g[i].reshape(1, GROUP_WIDTH), gmlp_ln_b[i].reshape(1, GROUP_WIDTH),
                                gmlp_w_s[i], bs_b)
        out_d = _hgrn2(h, hgrn_lb_logits, hgrn_norm_g[i].reshape(1, GROUP_WIDTH), bsz=bsz, seq=seq, layer_idx=i)
        xs = _out_proj_ln(xs, (out_a, out_b, out_c, out_d),
                          w_mix_out[i].astype(BF16).reshape(4, GROUP_WIDTH, d), lng[1], lnb[1])
        xs = _ffn_ln(xs, ffn2_w_in[i].astype(BF16), ffn2_w_out[i].astype(BF16), lng[2], lnb[2])
        xs = _ple_ln(xs, p.reshape(DEPTH * n, PLE_DIM), i, ple_w_gate[i].astype(BF16), ple_w_proj[i].astype(BF16),
                     lng[3], lnb[3])
    return xs.reshape(bsz, seq, d)
```

```python
import functools
import math

import numpy as np
import jax
import jax.numpy as jnp
from jax import lax
from jax.experimental import pallas as pl
from jax.experimental.pallas import tpu as pltpu

D_MODEL = 2048
DEPTH = 2
PLE_DIM = 256
D_FF = 5632
GROUP_WIDTH = 512

A_HEADS = 4
A_HEAD_DIM = 64
N_BUCKETS = 32
MAX_DISTANCE = 128

B_HEADS = 4
B_HEAD_DIM = 128
B_CHUNK = 128
ROPE_BASE = 10000.0

C_GROUPS = 4
C_CHUNK = 128

D_HEADS = 4
D_CHUNK = 64
D_LEVELS = 6

IN_COLS = 13 * GROUP_WIDTH
ALPHA = (2 * DEPTH) ** 0.25
LN_EPS = 1e-5
MASK_VALUE = -1e30
LB_FLOOR = 1e-30
LOG2E = math.log2(math.e)

LANES = 128
VMEM_LIMIT = 56 * 1024 * 1024
FFN_VMEM_LIMIT = 63 * 1024 * 1024

F32 = jnp.float32
BF16 = jnp.bfloat16


def _dot(a, b):
    return jnp.dot(a, b, preferred_element_type=F32)


def _dot_nt(a, b):
    return lax.dot_general(a, b, (((1,), (1,)), ((), ())), preferred_element_type=F32)


def _sigmoid(x):
    return 1.0 / (1.0 + jnp.exp(-x))


def _layer_norm(y, g, b):
    mu = jnp.mean(y, axis=-1, keepdims=True)
    d = y - mu
    var = jnp.mean(d * d, axis=-1, keepdims=True)
    return d * lax.rsqrt(var + LN_EPS) * g + b


def _params(*sem, vmem=VMEM_LIMIT):
    return pltpu.CompilerParams(dimension_semantics=sem, vmem_limit_bytes=vmem)


def _row(ref, r):
    return ref[r:r + 1, :]


def _ffn_kernel(x_ref, wg_ref, wu_ref, wo_ref, g_ref, b_ref, o_ref, xb_ref, *, ln_row):
    j = pl.program_id(1)

    @pl.when(j == 0)
    def _():
        xb_ref[...] = x_ref[...].astype(BF16)
        o_ref[...] = jnp.zeros_like(o_ref)

    xb = xb_ref[...]
    hg = _dot(xb, wg_ref[...])
    hu = _dot(xb, wu_ref[...])
    a = (hg * _sigmoid(hg) * hu).astype(BF16)
    o_ref[...] += _dot(a, wo_ref[...])

    @pl.when(j == pl.num_programs(1) - 1)
    def _():
        y = ALPHA * x_ref[...] + 0.5 * o_ref[...]
        o_ref[...] = _layer_norm(y, _row(g_ref, ln_row), _row(b_ref, ln_row))


def _ffn_ln(x, w_in, w_out, g, b, *, layer, ln_row, tm=1024, tf=512):
    n, d = x.shape
    f = w_out.shape[1]
    nf = f // tf
    ln_spec = pl.BlockSpec(g.shape, lambda i, j: (0, 0))
    return pl.pallas_call(
        functools.partial(_ffn_kernel, ln_row=ln_row),
        grid=(n // tm, nf),
        in_specs=[
            pl.BlockSpec((tm, d), lambda i, j: (i, 0)),
            pl.BlockSpec((None, d, tf), lambda i, j: (layer, 0, j)),
            pl.BlockSpec((None, d, tf), lambda i, j: (layer, 0, j + nf)),
            pl.BlockSpec((None, tf, d), lambda i, j: (layer, j, 0)),
            ln_spec, ln_spec,
        ],
        out_specs=pl.BlockSpec((tm, d), lambda i, j: (i, 0)),
        out_shape=jax.ShapeDtypeStruct((n, d), F32),
        scratch_shapes=[pltpu.VMEM((tm, d), BF16)],
        compiler_params=_params("parallel", "arbitrary", vmem=FFN_VMEM_LIMIT),
        name="ffn_ln",
    )(x, w_in, w_in, w_out, g, b)


def _in_proj_kernel(x_ref, w_ref, o_ref, xb_ref):
    @pl.when(pl.program_id(1) == 0)
    def _():
        xb_ref[...] = x_ref[...].astype(BF16)

    o_ref[...] = _dot(xb_ref[...], w_ref[...])


def _in_proj(x, w, *, layer, tm=2048, tn=512):
    n, d = x.shape
    c = w.shape[2]
    return pl.pallas_call(
        _in_proj_kernel,
        grid=(n // tm, c // tn),
        in_specs=[
            pl.BlockSpec((tm, d), lambda i, j: (i, 0)),
            pl.BlockSpec((None, d, tn), lambda i, j: (layer, 0, j)),
        ],
        out_specs=pl.BlockSpec((tm, tn), lambda i, j: (i, j)),
        out_shape=jax.ShapeDtypeStruct((n, c), F32),
        scratch_shapes=[pltpu.VMEM((tm, d), BF16)],
        compiler_params=_params("parallel", "arbitrary"),
        name="in_proj",
    )(x, w)


def _bias_tiles_kernel(rb_ref, o_ref, *, t):
    h = pl.program_id(0)
    j = lax.broadcasted_iota(jnp.int32, (t, t), 0)
    i = lax.broadcasted_iota(jnp.int32, (t, t), 1)
    max_exact = N_BUCKETS // 2
    for sel in range(2):
        n = jnp.maximum(i - j + sel * t, 0)
        nf = jnp.maximum(n, 1).astype(F32)
        large = max_exact + (jnp.log(nf / max_exact) / math.log(MAX_DISTANCE / max_exact)
                             * (N_BUCKETS - max_exact)).astype(jnp.int32)
        large = jnp.minimum(large, N_BUCKETS - 1)
        bucket = jnp.where(n < max_exact, n, large)
        val = jnp.zeros((t, t), F32)
        for bkt in range(N_BUCKETS):
            val = jnp.where(bucket == bkt, rb_ref[bkt, h] * LOG2E, val)
        if sel == 0:
            val = jnp.where(i >= j, val, MASK_VALUE)
        o_ref[0, sel] = val


def _bias_tiles(rel_bias, t):
    return pl.pallas_call(
        functools.partial(_bias_tiles_kernel, t=t),
        grid=(A_HEADS,),
        in_specs=[pl.BlockSpec(memory_space=pltpu.SMEM)],
        out_specs=pl.BlockSpec((1, 2, t, t), lambda h: (h, 0, 0, 0)),
        out_shape=jax.ShapeDtypeStruct((A_HEADS, 2, t, t), F32),
        compiler_params=_params("arbitrary"),
        name="t5_bias_tiles",
    )(rel_bias)


def _attn_kernel(q_ref, k_ref, v_ref, bt_ref, rb_ref, lam_ref, ng_ref, o_ref,
                 kb_ref, vt_ref, qst_ref, sa_ref, sb_ref, m_ref, l_ref, acc_ref, *, t, layer, lam_init):
    h = pl.program_id(1)
    i = pl.program_id(2)

    @pl.when(i == 0)
    def _():
        def fill(c, carry):
            rows = pl.ds(pl.multiple_of(c * t, t), t)
            kb_ref[c] = k_ref[rows, :].astype(BF16)
            vt_ref[c] = v_ref[rows, :].T.astype(BF16)
            return carry

        lax.fori_loop(0, kb_ref.shape[0], fill, 0)

    qt = (q_ref[...] * (A_HEAD_DIM ** -0.5 * LOG2E)).T
    feat = lax.broadcasted_iota(jnp.int32, qt.shape, 0)
    qst_ref[:, 0:t] = jnp.where(feat < A_HEAD_DIM, qt, 0.0).astype(BF16)
    qst_ref[:, t:2 * t] = jnp.where(feat >= A_HEAD_DIM, qt, 0.0).astype(BF16)
    m_ref[...] = jnp.full(m_ref.shape, MASK_VALUE, F32)
    l_ref[...] = jnp.zeros_like(l_ref)
    acc_ref[...] = jnp.zeros_like(acc_ref)

    def scores(kj, s2_ref):
        s2_ref[...] = _dot(kb_ref[kj], qst_ref[...])

    def step(kj, s2_ref, tile_sel, const_bias):
        vt = vt_ref[kj]
        for c in range(2):
            s = s2_ref[:, c * t:(c + 1) * t]
            if tile_sel is not None:
                s = s + bt_ref[0, tile_sel]
            m_old = m_ref[c]
            smax = jnp.max(s, axis=0, keepdims=True)
            if const_bias is not None:
                smax = smax + const_bias
            m_new = jnp.maximum(m_old, smax)
            shift = m_new if const_bias is None else m_new - const_bias
            p = jnp.exp2(s - shift)
            corr = jnp.exp2(m_old - m_new)
            l_ref[c] = corr * l_ref[c] + jnp.sum(p, axis=0, keepdims=True)
            acc_ref[c] = corr * acc_ref[c] + _dot(vt, p.astype(BF16))
            m_ref[c] = m_new

    far_bias = rb_ref[N_BUCKETS - 1, h] * LOG2E

    n_far = i - 1
    odd_far = jnp.logical_and(i >= 2, n_far % 2 == 1)
    scores(0, sa_ref)

    def far_pair(pj, carry):
        scores(2 * pj + 1, sb_ref)
        step(2 * pj, sa_ref, None, far_bias)
        scores(2 * pj + 2, sa_ref)
        step(2 * pj + 1, sb_ref, None, far_bias)
        return carry

    lax.fori_loop(0, jnp.maximum(n_far, 0) // 2, far_pair, 0)

    @pl.when(odd_far)
    def _():
        scores(i - 1, sb_ref)
        step(i - 2, sa_ref, None, far_bias)
        scores(i, sa_ref)
        step(i - 1, sb_ref, 1, None)
        step(i, sa_ref, 0, None)

    @pl.when(jnp.logical_and(i >= 1, jnp.logical_not(odd_far)))
    def _():
        scores(i, sb_ref)
        step(i - 1, sa_ref, 1, None)
        step(i, sb_ref, 0, None)

    @pl.when(i == 0)
    def _():
        step(0, sa_ref, 0, None)

    lq1, lk1, lq2, lk2 = (_row(lam_ref, 4 * layer + r) for r in range(4))
    lam = (jnp.exp(jnp.sum(lq1 * lk1, axis=1, keepdims=True))
           - jnp.exp(jnp.sum(lq2 * lk2, axis=1, keepdims=True)) + lam_init)
    o = (acc_ref[0] / l_ref[0] - lam * (acc_ref[1] / l_ref[1])).T
    o = o * lax.rsqrt(jnp.mean(o * o, axis=-1, keepdims=True) + LN_EPS) * _row(ng_ref, layer)
    o_ref[...] = (o * (1.0 - lam_init)).astype(o_ref.dtype)


def _diff_attention(h, bias_tiles, rel_bias, diff_lambda, norm_g, *, bsz, seq, t, layer_idx):
    n = bsz * seq
    nq = seq // t
    lam_init = 0.8 - 0.6 * math.exp(-0.3 * layer_idx)
    kcol = GROUP_WIDTH // LANES
    return pl.pallas_call(
        functools.partial(_attn_kernel, t=t, layer=layer_idx, lam_init=lam_init),
        grid=(bsz, A_HEADS, nq),
        in_specs=[
            pl.BlockSpec((t, LANES), lambda b, hh, i: (b * nq + i, hh)),
            pl.BlockSpec((seq, LANES), lambda b, hh, i: (b, kcol + hh)),
            pl.BlockSpec((seq, LANES), lambda b, hh, i: (b, 2 * kcol + hh)),
            pl.BlockSpec((1, 2, t, t), lambda b, hh, i: (hh, 0, 0, 0)),
            pl.BlockSpec(memory_space=pltpu.SMEM),
            pl.BlockSpec(diff_lambda.shape, lambda b, hh, i: (0, 0)),
            pl.BlockSpec(norm_g.shape, lambda b, hh, i: (0, 0)),
        ],
        out_specs=pl.BlockSpec((t, LANES), lambda b, hh, i: (b * nq + i, hh)),
        out_shape=jax.ShapeDtypeStruct((n, GROUP_WIDTH), BF16),
        scratch_shapes=[
            pltpu.VMEM((nq, t, LANES), BF16), pltpu.VMEM((nq, LANES, t), BF16),
            pltpu.VMEM((LANES, 2 * t), BF16), pltpu.VMEM((t, 2 * t), F32), pltpu.VMEM((t, 2 * t), F32),
            pltpu.VMEM((2, 1, t), F32), pltpu.VMEM((2, 1, t), F32), pltpu.VMEM((2, LANES, t), F32),
        ],
        compiler_params=_params("parallel", "parallel", "arbitrary"),
        name="diff_attention",
    )(h, h, h, bias_tiles, rel_bias, diff_lambda, norm_g)


def _rope_kernel(pos_ref, inv_ref, cos_ref, sin_ref):
    ang = pos_ref[...].astype(F32) * inv_ref[...]
    lane = lax.broadcasted_iota(jnp.int32, ang.shape, 1)
    cos_ref[...] = jnp.cos(ang)
    sn = jnp.sin(ang)
    sin_ref[...] = jnp.where(lane < B_HEAD_DIM // 2, -sn, sn)


def _rope_tables(pos_b, inv_b, *, tr=1024):
    n = pos_b.shape[0]
    return pl.pallas_call(
        _rope_kernel,
        grid=(n // tr,),
        in_specs=[pl.BlockSpec((tr, LANES), lambda i: (i, 0)), pl.BlockSpec((1, LANES), lambda i: (0, 0))],
        out_specs=[pl.BlockSpec((tr, LANES), lambda i: (i, 0))] * 2,
        out_shape=[jax.ShapeDtypeStruct((n, LANES), F32)] * 2,
        compiler_params=_params("parallel"),
        name="rope_tables",
    )(pos_b, inv_b)


def _retention_consts():
    log_g = np.log(1.0 - 2.0 ** (-5.0 - np.arange(B_HEADS, dtype=np.float64)))
    j = np.arange(B_CHUNK, dtype=np.float64)
    diff = j[:, None] - j[None, :]
    decay_mask = np.where(diff >= 0, np.exp(log_g[:, None, None] * np.maximum(diff, 0.0)), 0.0)
    q_dec = np.exp(log_g[:, None] * (j[None, :] + 1.0))
    k_dec = np.exp(log_g[:, None] * (B_CHUNK - 1.0 - j[None, :]))
    chunk_dec = np.exp(log_g * B_CHUNK)
    bc = lambda a: np.ascontiguousarray(np.broadcast_to(a[:, :, None], (B_HEADS, B_CHUNK, LANES)))
    return (decay_mask.astype(np.float32), bc(q_dec).astype(np.float32), bc(k_dec).astype(np.float32),
            [float(c) for c in chunk_dec])


def _retention_kernel(q_ref, k_ref, v_ref, g_ref, cos_ref, sin_ref, dm_ref, qd_ref, kd_ref, o_ref, st_ref,
                      *, chunk_dec):
    @pl.when(pl.program_id(1) == 0)
    def _():
        st_ref[...] = jnp.zeros_like(st_ref)

    tr = q_ref.shape[0]
    half = B_HEAD_DIM // 2
    for c in range(tr // B_CHUNK):
        rows = slice(c * B_CHUNK, (c + 1) * B_CHUNK)
        cos = cos_ref[rows, :]
        sin = sin_ref[rows, :]
        for hh in range(B_HEADS):
            cols = slice(hh * B_HEAD_DIM, (hh + 1) * B_HEAD_DIM)
            q = q_ref[rows, cols]
            k = k_ref[rows, cols]
            q = q * cos + pltpu.roll(q, half, 1) * sin
            k = (k * cos + pltpu.roll(k, half, 1) * sin) * (B_HEAD_DIM ** -0.5)
            qb = q.astype(BF16)
            vb = v_ref[rows, cols].astype(BF16)
            scores = _dot_nt(qb, k.astype(BF16)) * dm_ref[hh]
            st = st_ref[hh]
            o = _dot(scores.astype(BF16), vb) + _dot(qb, st.astype(BF16)) * qd_ref[hh]
            st_ref[hh] = st * chunk_dec[hh] + _dot((k * kd_ref[hh]).T.astype(BF16), vb)
            mu = jnp.mean(o, axis=-1, keepdims=True)
            d = o - mu
            var = jnp.mean(d * d, axis=-1, keepdims=True)
            gate = g_ref[rows, cols]
            o_ref[rows, cols] = (d * lax.rsqrt(var + LN_EPS) * (gate * _sigmoid(gate))).astype(o_ref.dtype)


def _retention(h, cos_t, sin_t, *, bsz, seq, tr=512):
    n = bsz * seq
    ns = seq // tr
    dm, qd, kd, chunk_dec = _retention_consts()
    row = lambda b, s: b * ns + s
    const3 = pl.BlockSpec((B_HEADS, B_CHUNK, LANES), lambda b, s: (0, 0, 0))
    return pl.pallas_call(
        functools.partial(_retention_kernel, chunk_dec=chunk_dec),
        grid=(bsz, ns),
        in_specs=[
            pl.BlockSpec((tr, GROUP_WIDTH), lambda b, s: (row(b, s), 3)),
            pl.BlockSpec((tr, GROUP_WIDTH), lambda b, s: (row(b, s), 4)),
            pl.BlockSpec((tr, GROUP_WIDTH), lambda b, s: (row(b, s), 5)),
            pl.BlockSpec((tr, GROUP_WIDTH), lambda b, s: (row(b, s), 6)),
            pl.BlockSpec((tr, LANES), lambda b, s: (row(b, s), 0)),
            pl.BlockSpec((tr, LANES), lambda b, s: (row(b, s), 0)),
            const3, const3, const3,
        ],
        out_specs=pl.BlockSpec((tr, GROUP_WIDTH), lambda b, s: (row(b, s), 0)),
        out_shape=jax.ShapeDtypeStruct((n, GROUP_WIDTH), BF16),
        scratch_shapes=[pltpu.VMEM((B_HEADS, B_HEAD_DIM, B_HEAD_DIM), F32)],
        compiler_params=_params("parallel", "arbitrary"),
        name="retention",
    )(h, h, h, h, cos_t, sin_t, jnp.asarray(dm), jnp.asarray(qd), jnp.asarray(kd))


def _gelu(x):
    return 0.5 * x * (1.0 + lax.erf(x * (0.5 ** 0.5)))


def _gmlp_kernel(u_ref, v_ref, g_ref, b_ref, ws_ref, bs_ref, o_ref, *, layer):
    tr = u_ref.shape[0]
    v = _layer_norm(_gelu(v_ref[...]), _row(g_ref, layer), _row(b_ref, layer)).astype(BF16)
    ti = lax.broadcasted_iota(jnp.int32, (C_CHUNK, C_CHUNK), 0)
    si = lax.broadcasted_iota(jnp.int32, (C_CHUNK, C_CHUNK), 1)
    for gi in range(C_GROUPS):
        cols = slice(gi * LANES, (gi + 1) * LANES)
        w = jnp.where(ti >= si, ws_ref[gi], 0.0).astype(BF16)
        for c in range(tr // C_CHUNK):
            rows = slice(c * C_CHUNK, (c + 1) * C_CHUNK)
            mixed = _dot(w, v[rows, cols]) + bs_ref[gi]
            o_ref[rows, cols] = (_gelu(u_ref[rows, cols]) * mixed).astype(o_ref.dtype)


def _spatial_gating(h, ln_g, ln_b, w_s, bs_b, *, layer, tr=512):
    n = h.shape[0]
    const3 = pl.BlockSpec((None, C_GROUPS, C_CHUNK, LANES), lambda i: (layer, 0, 0, 0))
    vec = pl.BlockSpec(ln_g.shape, lambda i: (0, 0))
    return pl.pallas_call(
        functools.partial(_gmlp_kernel, layer=layer),
        grid=(n // tr,),
        in_specs=[
            pl.BlockSpec((tr, GROUP_WIDTH), lambda i: (i, 7)),
            pl.BlockSpec((tr, GROUP_WIDTH), lambda i: (i, 8)),
            vec, vec, const3, const3,
        ],
        out_specs=pl.BlockSpec((tr, GROUP_WIDTH), lambda i: (i, 0)),
        out_shape=jax.ShapeDtypeStruct((n, GROUP_WIDTH), BF16),
        compiler_params=_params("parallel"),
        name="spatial_gating",
    )(h, h, ln_g, ln_b, w_s, bs_b)


def _hgrn_consts():
    c = D_CHUNK
    t = np.arange(c)[:, None]
    r = np.arange(c)[None, :]
    ranges = [r <= t, r > t]
    masks = [r == t]
    m = c // 2
    while m >= 1:
        ref = (t // (2 * m)) * 2 * m + m - 1
        second = (t % (2 * m)) >= m
        ranges.append(np.where(second, (r > ref) & (r <= t), (r > t) & (r <= ref)))
        masks.append(((t // (2 * m)) == (r // (2 * m))) & second & ((r % (2 * m)) < m))
        m //= 2
    ranges = np.concatenate(ranges, axis=0)
    return (np.concatenate([ranges, ranges], axis=1).astype(np.float32), np.stack(masks).astype(np.float32))


def _hgrn_kernel(q_ref, z_ref, i_ref, g_ref, lbl_ref, ng_ref, rs_ref, pm_ref, o_ref,
                 st_ref, oc_ref, key_ref, ex_ref, *, layer_idx):
    @pl.when(pl.program_id(1) == 0)
    def _():
        st_ref[...] = jnp.zeros_like(st_ref)

    tr = q_ref.shape[0]
    cs = D_CHUNK

    logits = lbl_ref[...]
    e = jnp.exp(logits - jnp.max(logits, axis=0, keepdims=True))
    soft = e / jnp.sum(e, axis=0, keepdims=True)
    lb = jnp.sum(soft[0:layer_idx + 1], axis=0, keepdims=True) - soft[0:1]
    lb = jnp.maximum(lb, LB_FLOOR)

    for c in range(tr // cs):
        rows = slice(c * cs, (c + 1) * cs)
        z = z_ref[rows, :]
        ez = jnp.exp(-jnp.abs(z))
        r = 1.0 / (1.0 + ez)
        sig_pos = jnp.where(z >= 0, r, ez * r)
        sig_neg = jnp.where(z >= 0, ez * r, r)
        log2_f = jnp.log(sig_pos + lb * sig_neg) * LOG2E
        key_ref[rows, :] = (1.0 - lb) * sig_neg
        hi = log2_f.astype(BF16)
        lo = (log2_f - hi.astype(F32)).astype(BF16)
        ex_ref[c] = _dot(rs_ref[...], jnp.concatenate([hi, lo], axis=0))

    for c in range(tr // cs):
        rows = slice(c * cs, (c + 1) * cs)
        for hh in range(D_HEADS):
            cols = slice(hh * LANES, (hh + 1) * LANES)
            q = q_ref[rows, cols]
            k = key_ref[rows, cols]
            vb = i_ref[rows, cols].astype(BF16)
            e_all = ex_ref.at[c]
            b = e_all[0:cs, cols]
            rem = e_all[cs:2 * cs, cols]
            st = st_ref[hh]
            o = _dot_nt((q * jnp.exp2(b)).astype(BF16), st.astype(BF16))
            a = jnp.where(pm_ref[0] > 0, _dot_nt(q.astype(BF16), k.astype(BF16)), 0.0)
            for lv in range(D_LEVELS):
                w = jnp.exp2(e_all[(2 + lv) * cs:(3 + lv) * cs, cols])
                a = a + jnp.where(pm_ref[1 + lv] > 0, _dot_nt((q * w).astype(BF16), (k * w).astype(BF16)), 0.0)
            oc_ref[rows, cols] = o + _dot(a.astype(BF16), vb)
            kdec = (k * jnp.exp2(rem)).astype(BF16)
            st_ref[hh] = st * jnp.exp2(b[cs - 1:cs, :]) + _dot(i_ref[rows, cols].T.astype(BF16), kdec)

    o = oc_ref[...]
    o = o * lax.rsqrt(jnp.mean(o * o, axis=-1, keepdims=True) + LN_EPS) * _row(ng_ref, layer_idx)
    gate = g_ref[...]
    o_ref[...] = (o * (gate * _sigmoid(gate))).astype(o_ref.dtype)


def _hgrn2(h, lb_logits, norm_g, *, bsz, seq, layer_idx, tr=256):
    n = bsz * seq
    ns = seq // tr
    row = lambda b, s: b * ns + s
    range_sums, pair_masks = _hgrn_consts()
    return pl.pallas_call(
        functools.partial(_hgrn_kernel, layer_idx=layer_idx),
        grid=(bsz, ns),
        in_specs=[
            pl.BlockSpec((tr, GROUP_WIDTH), lambda b, s: (row(b, s), 9)),
            pl.BlockSpec((tr, GROUP_WIDTH), lambda b, s: (row(b, s), 10)),
            pl.BlockSpec((tr, GROUP_WIDTH), lambda b, s: (row(b, s), 11)),
            pl.BlockSpec((tr, GROUP_WIDTH), lambda b, s: (row(b, s), 12)),
            pl.BlockSpec((DEPTH, GROUP_WIDTH), lambda b, s: (0, 0)),
            pl.BlockSpec((DEPTH, GROUP_WIDTH), lambda b, s: (0, 0)),
            pl.BlockSpec(range_sums.shape, lambda b, s: (0, 0)),
            pl.BlockSpec(pair_masks.shape, lambda b, s: (0, 0, 0)),
        ],
        out_specs=pl.BlockSpec((tr, GROUP_WIDTH), lambda b, s: (row(b, s), 0)),
        out_shape=jax.ShapeDtypeStruct((n, GROUP_WIDTH), BF16),
        scratch_shapes=[pltpu.VMEM((D_HEADS, LANES, LANES), F32), pltpu.VMEM((tr, GROUP_WIDTH), F32),
                        pltpu.VMEM((tr, GROUP_WIDTH), F32),
                        pltpu.VMEM((tr // D_CHUNK, range_sums.shape[0], GROUP_WIDTH), F32)],
        compiler_params=_params("parallel", "arbitrary"),
        name="hgrn2",
    )(h, h, h, h, lb_logits, norm_g, jnp.asarray(range_sums, BF16), jnp.asarray(pair_masks))


def _out_proj_kernel(x_ref, a_ref, b_ref, c_ref, d_ref, w_ref, g_ref, bb_ref, o_ref, *, ln_row):
    acc = _dot(a_ref[...], w_ref[0])
    acc += _dot(b_ref[...], w_ref[1])
    acc += _dot(c_ref[...], w_ref[2])
    acc += _dot(d_ref[...], w_ref[3])
    o_ref[...] = _layer_norm(ALPHA * x_ref[...] + acc, _row(g_ref, ln_row), _row(bb_ref, ln_row))


def _out_proj_ln(x, parts, w, g, b, *, layer, ln_row, tm=512):
    n, d = x.shape
    part = pl.BlockSpec((tm, GROUP_WIDTH), lambda i: (i, 0))
    vec = pl.BlockSpec(g.shape, lambda i: (0, 0))
    return pl.pallas_call(
        functools.partial(_out_proj_kernel, ln_row=ln_row),
        grid=(n // tm,),
        in_specs=[pl.BlockSpec((tm, d), lambda i: (i, 0)), part, part, part, part,
                  pl.BlockSpec((None, 4, GROUP_WIDTH, d), lambda i: (layer, 0, 0, 0)), vec, vec],
        out_specs=pl.BlockSpec((tm, d), lambda i: (i, 0)),
        out_shape=jax.ShapeDtypeStruct((n, d), F32),
        compiler_params=_params("parallel"),
        name="out_proj_ln",
    )(x, *parts, w, g, b)


def _ple_kernel(x_ref, p_ref, wg_ref, we_ref, g_ref, b_ref, o_ref, *, ln_row):
    x = x_ref[...]
    gate = _sigmoid(_dot(x.astype(BF16), wg_ref[...]))
    emb = _dot(p_ref[...].astype(BF16), we_ref[...])
    o_ref[...] = _layer_norm(ALPHA * x + gate * emb, _row(g_ref, ln_row), _row(b_ref, ln_row))


def _ple_ln(x, p_all, wg, we, g, b, *, layer, ln_row, tm=512):
    n, d = x.shape
    nt = n // tm
    vec = pl.BlockSpec(g.shape, lambda i: (0, 0))
    return pl.pallas_call(
        functools.partial(_ple_kernel, ln_row=ln_row),
        grid=(nt,),
        in_specs=[pl.BlockSpec((tm, d), lambda i: (i, 0)),
                  pl.BlockSpec((tm, PLE_DIM), lambda i: (layer * nt + i, 0)),
                  pl.BlockSpec((None, d, d), lambda i: (layer, 0, 0)),
                  pl.BlockSpec((None, PLE_DIM, d), lambda i: (layer, 0, 0)), vec, vec],
        out_specs=pl.BlockSpec((tm, d), lambda i: (i, 0)),
        out_shape=jax.ShapeDtypeStruct((n, d), F32),
        compiler_params=_params("parallel"),
        name="ple_ln",
    )(x, p_all, wg, we, g, b)


def _mix_in_weight(w):
    nl, d, _ = w.shape

    def even_odd(blk):
        blk = blk.reshape(nl, d, B_HEADS, B_HEAD_DIM // 2, 2)
        return jnp.swapaxes(blk, 3, 4).reshape(nl, d, GROUP_WIDTH)

    q0, k0, v0 = 3 * GROUP_WIDTH, 4 * GROUP_WIDTH, 5 * GROUP_WIDTH
    parts = [w[:, :, :q0], even_odd(w[:, :, q0:k0]), even_odd(w[:, :, k0:v0]), w[:, :, v0:]]
    return jnp.concatenate(parts, axis=2).astype(BF16)


def kernel(x, p, positions, ffn1_w_in, ffn1_w_out, w_mix_in, w_mix_out, rel_bias, diff_lambda, diff_norm_g,
           gmlp_ln_g, gmlp_ln_b, gmlp_w_s, gmlp_b_s, hgrn_lb_logits, hgrn_norm_g, ffn2_w_in, ffn2_w_out,
           ple_w_gate, ple_w_proj, ln_g, ln_b):
    bsz, seq, d = x.shape
    n = bsz * seq
    t_attn = min(512, seq)
    xs = x.reshape(n, d)

    pos_b = jnp.broadcast_to(positions.reshape(n, 1), (n, LANES))
    inv = ROPE_BASE ** (-jnp.linspace(0.0, 1.0, B_HEAD_DIM // 2, dtype=F32))
    inv_b = jnp.concatenate([inv, inv]).reshape(1, LANES)
    cos_t, sin_t = _rope_tables(pos_b, inv_b, tr=min(1024, n))
    bias_tiles = _bias_tiles(rel_bias, t_attn)

    ffn1_in, ffn1_out = ffn1_w_in.astype(BF16), ffn1_w_out.astype(BF16)
    ffn2_in, ffn2_out = ffn2_w_in.astype(BF16), ffn2_w_out.astype(BF16)
    mix_in = _mix_in_weight(w_mix_in)
    mix_out = w_mix_out.astype(BF16).reshape(DEPTH, 4, GROUP_WIDTH, d)
    ple_gate, ple_proj = ple_w_gate.astype(BF16), ple_w_proj.astype(BF16)
    p_all = p.reshape(DEPTH * n, PLE_DIM)
    lng, lnb = ln_g.reshape(DEPTH * 4, d), ln_b.reshape(DEPTH * 4, d)
    lam_all = diff_lambda.reshape(DEPTH * 4, A_HEAD_DIM)
    bs_b = jnp.broadcast_to(gmlp_b_s[:, :, :, None], (DEPTH, C_GROUPS, C_CHUNK, LANES))

    for i in range(DEPTH):
        xs = _ffn_ln(xs, ffn1_in, ffn1_out, lng, lnb, layer=i, ln_row=4 * i)
        h = _in_proj(xs, mix_in, layer=i)
        out_a = _diff_attention(h, bias_tiles, rel_bias, lam_all, diff_norm_g, bsz=bsz, seq=seq, t=t_attn, layer_idx=i)
        out_b = _retention(h, cos_t, sin_t, bsz=bsz, seq=seq)
        out_c = _spatial_gating(h, gmlp_ln_g, gmlp_ln_b, gmlp_w_s, bs_b, layer=i)
        out_d = _hgrn2(h, hgrn_lb_logits, hgrn_norm_g, bsz=bsz, seq=seq, layer_idx=i)
        xs = _out_proj_ln(xs, (out_a, out_b, out_c, out_d), mix_out, lng, lnb, layer=i, ln_row=4 * i + 1)
        xs = _ffn_ln(xs, ffn2_in, ffn2_out, lng, lnb, layer=i, ln_row=4 * i + 2)
        xs = _ple_ln(xs, p_all, ple_gate, ple_proj, lng, lnb, layer=i, ln_row=4 * i + 3)
    return xs.reshape(bsz, seq, d)
```

```python
import functools
import math

import numpy as np
import jax
import jax.numpy as jnp
from jax import lax
from jax.experimental import pallas as pl
from jax.experimental.pallas import tpu as pltpu

D_MODEL = 2048
DEPTH = 2
PLE_DIM = 256
D_FF = 5632
GROUP_WIDTH = 512

A_HEADS = 4
A_HEAD_DIM = 64
N_BUCKETS = 32
MAX_DISTANCE = 128

B_HEADS = 4
B_HEAD_DIM = 128
B_CHUNK = 128
ROPE_BASE = 10000.0

C_GROUPS = 4
C_CHUNK = 128

D_HEADS = 4
D_CHUNK = 64
D_LEVELS = 6

IN_COLS = 13 * GROUP_WIDTH
ALPHA = (2 * DEPTH) ** 0.25
LN_EPS = 1e-5
MASK_VALUE = -1e30
LB_FLOOR = 1e-30
LOG2E = math.log2(math.e)

LANES = 128
VMEM_LIMIT = 56 * 1024 * 1024
FFN_VMEM_LIMIT = 63 * 1024 * 1024

F32 = jnp.float32
BF16 = jnp.bfloat16


def _dot(a, b):
    return jnp.dot(a, b, preferred_element_type=F32)


def _dot_nt(a, b):
    return lax.dot_general(a, b, (((1,), (1,)), ((), ())), preferred_element_type=F32)


def _sigmoid(x):
    return 1.0 / (1.0 + jnp.exp(-x))


def _layer_norm(y, g, b):
    mu = jnp.mean(y, axis=-1, keepdims=True)
    d = y - mu
    var = jnp.mean(d * d, axis=-1, keepdims=True)
    return d * lax.rsqrt(var + LN_EPS) * g + b


def _tiles(total, tile):
    assert total % tile == 0, f"{total} rows/columns do not split into tiles of {tile}"
    return total // tile


def _params(*sem, vmem=VMEM_LIMIT):
    return pltpu.CompilerParams(dimension_semantics=sem, vmem_limit_bytes=vmem)


def _row(ref, r):
    return ref[r:r + 1, :]


def _ffn_kernel(x_ref, wg_ref, wu_ref, wo_ref, g_ref, b_ref, o_ref, xb_ref, *, ln_row):
    j = pl.program_id(1)

    @pl.when(j == 0)
    def _():
        xb_ref[...] = x_ref[...].astype(BF16)
        o_ref[...] = jnp.zeros_like(o_ref)

    xb = xb_ref[...]
    hg = _dot(xb, wg_ref[...])
    hu = _dot(xb, wu_ref[...])
    a = (hg * _sigmoid(hg) * hu).astype(BF16)
    o_ref[...] += _dot(a, wo_ref[...])

    @pl.when(j == pl.num_programs(1) - 1)
    def _():
        y = ALPHA * x_ref[...] + 0.5 * o_ref[...]
        o_ref[...] = _layer_norm(y, _row(g_ref, ln_row), _row(b_ref, ln_row))


def _ffn_ln(x, w_in, w_out, g, b, *, layer, ln_row, tm=1024, tf=512):
    n, d = x.shape
    f = w_out.shape[1]
    nf = _tiles(f, tf)
    ln_spec = pl.BlockSpec(g.shape, lambda i, j: (0, 0))
    return pl.pallas_call(
        functools.partial(_ffn_kernel, ln_row=ln_row),
        grid=(_tiles(n, tm), nf),
        in_specs=[
            pl.BlockSpec((tm, d), lambda i, j: (i, 0)),
            pl.BlockSpec((None, d, tf), lambda i, j: (layer, 0, j)),
            pl.BlockSpec((None, d, tf), lambda i, j: (layer, 0, j + nf)),
            pl.BlockSpec((None, tf, d), lambda i, j: (layer, j, 0)),
            ln_spec, ln_spec,
        ],
        out_specs=pl.BlockSpec((tm, d), lambda i, j: (i, 0)),
        out_shape=jax.ShapeDtypeStruct((n, d), F32),
        scratch_shapes=[pltpu.VMEM((tm, d), BF16)],
        compiler_params=_params("parallel", "arbitrary", vmem=FFN_VMEM_LIMIT),
        name="ffn_ln",
    )(x, w_in, w_in, w_out, g, b)


def _in_proj_kernel(x_ref, w_ref, o_ref, xb_ref):
    @pl.when(pl.program_id(1) == 0)
    def _():
        xb_ref[...] = x_ref[...].astype(BF16)

    o_ref[...] = _dot(xb_ref[...], w_ref[...])


def _in_proj(x, w, *, layer, tm=2048, tn=512):
    n, d = x.shape
    c = w.shape[2]
    return pl.pallas_call(
        _in_proj_kernel,
        grid=(_tiles(n, tm), _tiles(c, tn)),
        in_specs=[
            pl.BlockSpec((tm, d), lambda i, j: (i, 0)),
            pl.BlockSpec((None, d, tn), lambda i, j: (layer, 0, j)),
        ],
        out_specs=pl.BlockSpec((tm, tn), lambda i, j: (i, j)),
        out_shape=jax.ShapeDtypeStruct((n, c), F32),
        scratch_shapes=[pltpu.VMEM((tm, d), BF16)],
        compiler_params=_params("parallel", "arbitrary"),
        name="in_proj",
    )(x, w)


def _bias_tiles_kernel(rb_ref, o_ref, *, t):
    h = pl.program_id(0)
    j = lax.broadcasted_iota(jnp.int32, (t, t), 0)
    i = lax.broadcasted_iota(jnp.int32, (t, t), 1)
    max_exact = N_BUCKETS // 2
    for sel in range(2):
        n = jnp.maximum(i - j + sel * t, 0)
        nf = jnp.maximum(n, 1).astype(F32)
        large = max_exact + (jnp.log(nf / max_exact) / math.log(MAX_DISTANCE / max_exact)
                             * (N_BUCKETS - max_exact)).astype(jnp.int32)
        large = jnp.minimum(large, N_BUCKETS - 1)
        bucket = jnp.where(n < max_exact, n, large)
        val = jnp.zeros((t, t), F32)
        for bkt in range(N_BUCKETS):
            val = jnp.where(bucket == bkt, rb_ref[bkt, h] * LOG2E, val)
        if sel == 0:
            val = jnp.where(i >= j, val, MASK_VALUE)
        o_ref[0, sel] = val


def _bias_tiles(rel_bias, t):
    return pl.pallas_call(
        functools.partial(_bias_tiles_kernel, t=t),
        grid=(A_HEADS,),
        in_specs=[pl.BlockSpec(memory_space=pltpu.SMEM)],
        out_specs=pl.BlockSpec((1, 2, t, t), lambda h: (h, 0, 0, 0)),
        out_shape=jax.ShapeDtypeStruct((A_HEADS, 2, t, t), F32),
        compiler_params=_params("arbitrary"),
        name="t5_bias_tiles",
    )(rel_bias)


def _attn_kernel(q_ref, k_ref, v_ref, bt_ref, rb_ref, lam_ref, ng_ref, o_ref,
                 kb_ref, vt_ref, qst_ref, sa_ref, sb_ref, m_ref, l_ref, acc_ref, *, t, layer, lam_init):
    h = pl.program_id(1)
    nq = kb_ref.shape[0]

    def fill(c, carry):
        rows = pl.ds(pl.multiple_of(c * t, t), t)
        kb_ref[c] = k_ref[rows, :].astype(BF16)
        vt_ref[c] = v_ref[rows, :].T.astype(BF16)
        return carry

    lax.fori_loop(0, nq, fill, 0)

    def scores(kj, s2_ref):
        s2_ref[...] = _dot(kb_ref[kj], qst_ref[...])

    def open_query_block(i):
        rows = pl.ds(pl.multiple_of(i * t, t), t)
        qt = (q_ref[rows, :] * (A_HEAD_DIM ** -0.5 * LOG2E)).T
        feat = lax.broadcasted_iota(jnp.int32, qt.shape, 0)
        qst_ref[:, 0:t] = jnp.where(feat < A_HEAD_DIM, qt, 0.0).astype(BF16)
        qst_ref[:, t:2 * t] = jnp.where(feat >= A_HEAD_DIM, qt, 0.0).astype(BF16)
        scores(0, sa_ref)

    def step(kj, s2_ref, tile_sel, const_bias):
        vt = vt_ref[kj]
        for c in range(2):
            s = s2_ref[:, c * t:(c + 1) * t]
            if tile_sel is not None:
                s = s + bt_ref[0, tile_sel]
            m_old = m_ref[c]
            smax = jnp.max(s, axis=0, keepdims=True)
            if const_bias is not None:
                smax = smax + const_bias
            m_new = jnp.maximum(m_old, smax)
            shift = m_new if const_bias is None else m_new - const_bias
            p = jnp.exp2(s - shift)
            corr = jnp.exp2(m_old - m_new)
            l_ref[c] = corr * l_ref[c] + jnp.sum(p, axis=0, keepdims=True)
            acc_ref[c] = corr * acc_ref[c] + _dot(vt, p.astype(BF16))
            m_ref[c] = m_new

    far_bias = rb_ref[N_BUCKETS - 1, h] * LOG2E

    lq1, lk1, lq2, lk2 = (_row(lam_ref, 4 * layer + r) for r in range(4))
    lam = (jnp.exp(jnp.sum(lq1 * lk1, axis=1, keepdims=True))
           - jnp.exp(jnp.sum(lq2 * lk2, axis=1, keepdims=True)) + lam_init)

    def query_block(i, carry):
        m_ref[...] = jnp.full(m_ref.shape, MASK_VALUE, F32)
        l_ref[...] = jnp.zeros_like(l_ref)
        acc_ref[...] = jnp.zeros_like(acc_ref)

        n_far = i - 1
        odd_far = jnp.logical_and(i >= 2, n_far % 2 == 1)

        def far_pair(pj, inner):
            scores(2 * pj + 1, sb_ref)
            step(2 * pj, sa_ref, None, far_bias)
            scores(2 * pj + 2, sa_ref)
            step(2 * pj + 1, sb_ref, None, far_bias)
            return inner

        lax.fori_loop(0, jnp.maximum(n_far, 0) // 2, far_pair, 0)

        @pl.when(odd_far)
        def _():
            scores(i - 1, sb_ref)
            step(i - 2, sa_ref, None, far_bias)
            scores(i, sa_ref)
            step(i - 1, sb_ref, 1, None)
            step(i, sa_ref, 0, None)

        @pl.when(jnp.logical_and(i >= 1, jnp.logical_not(odd_far)))
        def _():
            scores(i, sb_ref)
            step(i - 1, sa_ref, 1, None)
            step(i, sb_ref, 0, None)

        @pl.when(i == 0)
        def _():
            step(0, sa_ref, 0, None)

        open_query_block(jnp.minimum(i + 1, nq - 1))
        o = (acc_ref[0] / l_ref[0] - lam * (acc_ref[1] / l_ref[1])).T
        o = o * lax.rsqrt(jnp.mean(o * o, axis=-1, keepdims=True) + LN_EPS) * _row(ng_ref, layer)
        o_ref[pl.ds(pl.multiple_of(i * t, t), t), :] = (o * (1.0 - lam_init)).astype(o_ref.dtype)
        return carry

    open_query_block(0)
    lax.fori_loop(0, nq, query_block, 0)


def _diff_attention(h, bias_tiles, rel_bias, diff_lambda, norm_g, *, bsz, seq, t, layer_idx):
    n = bsz * seq
    nq = _tiles(seq, t)
    lam_init = 0.8 - 0.6 * math.exp(-0.3 * layer_idx)
    kcol = GROUP_WIDTH // LANES
    return pl.pallas_call(
        functools.partial(_attn_kernel, t=t, layer=layer_idx, lam_init=lam_init),
        grid=(bsz, A_HEADS),
        in_specs=[
            pl.BlockSpec((seq, LANES), lambda b, hh: (b, hh)),
            pl.BlockSpec((seq, LANES), lambda b, hh: (b, kcol + hh)),
            pl.BlockSpec((seq, LANES), lambda b, hh: (b, 2 * kcol + hh)),
            pl.BlockSpec((1, 2, t, t), lambda b, hh: (hh, 0, 0, 0)),
            pl.BlockSpec(memory_space=pltpu.SMEM),
            pl.BlockSpec(diff_lambda.shape, lambda b, hh: (0, 0)),
            pl.BlockSpec(norm_g.shape, lambda b, hh: (0, 0)),
        ],
        out_specs=pl.BlockSpec((seq, LANES), lambda b, hh: (b, hh)),
        out_shape=jax.ShapeDtypeStruct((n, GROUP_WIDTH), BF16),
        scratch_shapes=[
            pltpu.VMEM((nq, t, LANES), BF16), pltpu.VMEM((nq, LANES, t), BF16),
            pltpu.VMEM((LANES, 2 * t), BF16), pltpu.VMEM((t, 2 * t), F32), pltpu.VMEM((t, 2 * t), F32),
            pltpu.VMEM((2, 1, t), F32), pltpu.VMEM((2, 1, t), F32), pltpu.VMEM((2, LANES, t), F32),
        ],
        compiler_params=_params("parallel", "parallel"),
        name="diff_attention",
    )(h, h, h, bias_tiles, rel_bias, diff_lambda, norm_g)


def _rope_kernel(pos_ref, inv_ref, cos_ref, sin_ref):
    ang = pos_ref[...].astype(F32) * inv_ref[...]
    lane = lax.broadcasted_iota(jnp.int32, ang.shape, 1)
    cos_ref[...] = jnp.cos(ang)
    sn = jnp.sin(ang)
    sin_ref[...] = jnp.where(lane < B_HEAD_DIM // 2, -sn, sn)


def _rope_tables(pos_b, inv_b, *, tr=1024):
    n = pos_b.shape[0]
    return pl.pallas_call(
        _rope_kernel,
        grid=(_tiles(n, tr),),
        in_specs=[pl.BlockSpec((tr, LANES), lambda i: (i, 0)), pl.BlockSpec((1, LANES), lambda i: (0, 0))],
        out_specs=[pl.BlockSpec((tr, LANES), lambda i: (i, 0))] * 2,
        out_shape=[jax.ShapeDtypeStruct((n, LANES), F32)] * 2,
        compiler_params=_params("parallel"),
        name="rope_tables",
    )(pos_b, inv_b)


def _retention_consts():
    log_g = np.log(1.0 - 2.0 ** (-5.0 - np.arange(B_HEADS, dtype=np.float64)))
    j = np.arange(B_CHUNK, dtype=np.float64)
    diff = j[:, None] - j[None, :]
    decay_mask = np.where(diff >= 0, np.exp(log_g[:, None, None] * np.maximum(diff, 0.0)), 0.0)
    q_dec = np.exp(log_g[:, None] * (j[None, :] + 1.0))
    k_dec = np.exp(log_g[:, None] * (B_CHUNK - 1.0 - j[None, :]))
    chunk_dec = np.exp(log_g * B_CHUNK)
    bc = lambda a: np.ascontiguousarray(np.broadcast_to(a[:, :, None], (B_HEADS, B_CHUNK, LANES)))
    return (decay_mask.astype(np.float32), bc(q_dec).astype(np.float32), bc(k_dec).astype(np.float32),
            [float(c) for c in chunk_dec])


def _retention_kernel(q_ref, k_ref, v_ref, g_ref, cos_ref, sin_ref, dm_ref, qd_ref, kd_ref, o_ref, st_ref,
                      *, chunk_dec):
    @pl.when(pl.program_id(1) == 0)
    def _():
        st_ref[...] = jnp.zeros_like(st_ref)

    tr = q_ref.shape[0]
    half = B_HEAD_DIM // 2
    for c in range(tr // B_CHUNK):
        rows = slice(c * B_CHUNK, (c + 1) * B_CHUNK)
        cos = cos_ref[rows, :]
        sin = sin_ref[rows, :]
        for hh in range(B_HEADS):
            cols = slice(hh * B_HEAD_DIM, (hh + 1) * B_HEAD_DIM)
            q = q_ref[rows, cols]
            k = k_ref[rows, cols]
            q = q * cos + pltpu.roll(q, half, 1) * sin
            k = (k * cos + pltpu.roll(k, half, 1) * sin) * (B_HEAD_DIM ** -0.5)
            qb = q.astype(BF16)
            vb = v_ref[rows, cols].astype(BF16)
            scores = _dot_nt(qb, k.astype(BF16)) * dm_ref[hh]
            st = st_ref[hh]
            o = _dot(scores.astype(BF16), vb) + _dot(qb, st.astype(BF16)) * qd_ref[hh]
            st_ref[hh] = st * chunk_dec[hh] + _dot((k * kd_ref[hh]).T.astype(BF16), vb)
            mu = jnp.mean(o, axis=-1, keepdims=True)
            d = o - mu
            var = jnp.mean(d * d, axis=-1, keepdims=True)
            gate = g_ref[rows, cols]
            o_ref[rows, cols] = (d * lax.rsqrt(var + LN_EPS) * (gate * _sigmoid(gate))).astype(o_ref.dtype)


def _retention(h, cos_t, sin_t, *, bsz, seq, tr=512):
    n = bsz * seq
    ns = _tiles(seq, tr)
    dm, qd, kd, chunk_dec = _retention_consts()
    row = lambda b, s: b * ns + s
    const3 = pl.BlockSpec((B_HEADS, B_CHUNK, LANES), lambda b, s: (0, 0, 0))
    return pl.pallas_call(
        functools.partial(_retention_kernel, chunk_dec=chunk_dec),
        grid=(bsz, ns),
        in_specs=[
            pl.BlockSpec((tr, GROUP_WIDTH), lambda b, s: (row(b, s), 3)),
            pl.BlockSpec((tr, GROUP_WIDTH), lambda b, s: (row(b, s), 4)),
            pl.BlockSpec((tr, GROUP_WIDTH), lambda b, s: (row(b, s), 5)),
            pl.BlockSpec((tr, GROUP_WIDTH), lambda b, s: (row(b, s), 6)),
            pl.BlockSpec((tr, LANES), lambda b, s: (row(b, s), 0)),
            pl.BlockSpec((tr, LANES), lambda b, s: (row(b, s), 0)),
            const3, const3, const3,
        ],
        out_specs=pl.BlockSpec((tr, GROUP_WIDTH), lambda b, s: (row(b, s), 0)),
        out_shape=jax.ShapeDtypeStruct((n, GROUP_WIDTH), BF16),
        scratch_shapes=[pltpu.VMEM((B_HEADS, B_HEAD_DIM, B_HEAD_DIM), F32)],
        compiler_params=_params("parallel", "arbitrary"),
        name="retention",
    )(h, h, h, h, cos_t, sin_t, jnp.asarray(dm), jnp.asarray(qd), jnp.asarray(kd))


def _gelu(x):
    return 0.5 * x * (1.0 + lax.erf(x * (0.5 ** 0.5)))


def _gmlp_kernel(u_ref, v_ref, g_ref, b_ref, ws_ref, bs_ref, o_ref, *, layer):
    tr = u_ref.shape[0]
    v = _layer_norm(_gelu(v_ref[...]), _row(g_ref, layer), _row(b_ref, layer)).astype(BF16)
    ti = lax.broadcasted_iota(jnp.int32, (C_CHUNK, C_CHUNK), 0)
    si = lax.broadcasted_iota(jnp.int32, (C_CHUNK, C_CHUNK), 1)
    for gi in range(C_GROUPS):
        cols = slice(gi * LANES, (gi + 1) * LANES)
        w = jnp.where(ti >= si, ws_ref[gi], 0.0).astype(BF16)
        for c in range(tr // C_CHUNK):
            rows = slice(c * C_CHUNK, (c + 1) * C_CHUNK)
            mixed = _dot(w, v[rows, cols]) + bs_ref[gi]
            o_ref[rows, cols] = (_gelu(u_ref[rows, cols]) * mixed).astype(o_ref.dtype)


def _spatial_gating(h, ln_g, ln_b, w_s, bs_b, *, layer, tr=512):
    n = h.shape[0]
    const3 = pl.BlockSpec((None, C_GROUPS, C_CHUNK, LANES), lambda i: (layer, 0, 0, 0))
    vec = pl.BlockSpec(ln_g.shape, lambda i: (0, 0))
    return pl.pallas_call(
        functools.partial(_gmlp_kernel, layer=layer),
        grid=(_tiles(n, tr),),
        in_specs=[
            pl.BlockSpec((tr, GROUP_WIDTH), lambda i: (i, 7)),
            pl.BlockSpec((tr, GROUP_WIDTH), lambda i: (i, 8)),
            vec, vec, const3, const3,
        ],
        out_specs=pl.BlockSpec((tr, GROUP_WIDTH), lambda i: (i, 0)),
        out_shape=jax.ShapeDtypeStruct((n, GROUP_WIDTH), BF16),
        compiler_params=_params("parallel"),
        name="spatial_gating",
    )(h, h, ln_g, ln_b, w_s, bs_b)


def _hgrn_consts():
    c = D_CHUNK
    t = np.arange(c)[:, None]
    r = np.arange(c)[None, :]
    ranges = [r <= t, r > t]
    masks = [r == t]
    m = c // 2
    while m >= 1:
        ref = (t // (2 * m)) * 2 * m + m - 1
        second = (t % (2 * m)) >= m
        ranges.append(np.where(second, (r > ref) & (r <= t), (r > t) & (r <= ref)))
        masks.append(((t // (2 * m)) == (r // (2 * m))) & second & ((r % (2 * m)) < m))
        m //= 2
    ranges = np.concatenate(ranges, axis=0)
    return (np.concatenate([ranges, ranges], axis=1).astype(np.float32), np.stack(masks).astype(np.float32))


def _hgrn_kernel(q_ref, z_ref, i_ref, g_ref, lbl_ref, ng_ref, rs_ref, pm_ref, o_ref,
                 st_ref, oc_ref, key_ref, ex_ref, *, layer_idx):
    @pl.when(pl.program_id(1) == 0)
    def _():
        st_ref[...] = jnp.zeros_like(st_ref)

    tr = q_ref.shape[0]
    cs = D_CHUNK

    logits = lbl_ref[...]
    e = jnp.exp(logits - jnp.max(logits, axis=0, keepdims=True))
    soft = e / jnp.sum(e, axis=0, keepdims=True)
    lb = jnp.sum(soft[0:layer_idx + 1], axis=0, keepdims=True) - soft[0:1]
    lb = jnp.maximum(lb, LB_FLOOR)

    for c in range(tr // cs):
        rows = slice(c * cs, (c + 1) * cs)
        z = z_ref[rows, :]
        ez = jnp.exp(-jnp.abs(z))
        r = 1.0 / (1.0 + ez)
        sig_pos = jnp.where(z >= 0, r, ez * r)
        sig_neg = jnp.where(z >= 0, ez * r, r)
        log2_f = jnp.log(sig_pos + lb * sig_neg) * LOG2E
        key_ref[rows, :] = (1.0 - lb) * sig_neg
        hi = log2_f.astype(BF16)
        lo = (log2_f - hi.astype(F32)).astype(BF16)
        ex_ref[c] = _dot(rs_ref[...], jnp.concatenate([hi, lo], axis=0))

    for c in range(tr // cs):
        rows = slice(c * cs, (c + 1) * cs)
        for hh in range(D_HEADS):
            cols = slice(hh * LANES, (hh + 1) * LANES)
            q = q_ref[rows, cols]
            k = key_ref[rows, cols]
            vb = i_ref[rows, cols].astype(BF16)
            e_all = ex_ref.at[c]
            b = e_all[0:cs, cols]
            rem = e_all[cs:2 * cs, cols]
            st = st_ref[hh]
            o = _dot_nt((q * jnp.exp2(b)).astype(BF16), st.astype(BF16))
            a = jnp.where(pm_ref[0] > 0, _dot_nt(q.astype(BF16), k.astype(BF16)), 0.0)
            for lv in range(D_LEVELS):
                w = jnp.exp2(e_all[(2 + lv) * cs:(3 + lv) * cs, cols])
                a = a + jnp.where(pm_ref[1 + lv] > 0, _dot_nt((q * w).astype(BF16), (k * w).astype(BF16)), 0.0)
            oc_ref[rows, cols] = o + _dot(a.astype(BF16), vb)
            kdec = (k * jnp.exp2(rem)).astype(BF16)
            st_ref[hh] = st * jnp.exp2(b[cs - 1:cs, :]) + _dot(i_ref[rows, cols].T.astype(BF16), kdec)

    o = oc_ref[...]
    o = o * lax.rsqrt(jnp.mean(o * o, axis=-1, keepdims=True) + LN_EPS) * _row(ng_ref, layer_idx)
    gate = g_ref[...]
    o_ref[...] = (o * (gate * _sigmoid(gate))).astype(o_ref.dtype)


def _hgrn2(h, lb_logits, norm_g, *, bsz, seq, layer_idx, tr=512):
    n = bsz * seq
    ns = _tiles(seq, tr)
    row = lambda b, s: b * ns + s
    range_sums, pair_masks = _hgrn_consts()
    return pl.pallas_call(
        functools.partial(_hgrn_kernel, layer_idx=layer_idx),
        grid=(bsz, ns),
        in_specs=[
            pl.BlockSpec((tr, GROUP_WIDTH), lambda b, s: (row(b, s), 9)),
            pl.BlockSpec((tr, GROUP_WIDTH), lambda b, s: (row(b, s), 10)),
            pl.BlockSpec((tr, GROUP_WIDTH), lambda b, s: (row(b, s), 11)),
            pl.BlockSpec((tr, GROUP_WIDTH), lambda b, s: (row(b, s), 12)),
            pl.BlockSpec((DEPTH, GROUP_WIDTH), lambda b, s: (0, 0)),
            pl.BlockSpec((DEPTH, GROUP_WIDTH), lambda b, s: (0, 0)),
            pl.BlockSpec(range_sums.shape, lambda b, s: (0, 0)),
            pl.BlockSpec(pair_masks.shape, lambda b, s: (0, 0, 0)),
        ],
        out_specs=pl.BlockSpec((tr, GROUP_WIDTH), lambda b, s: (row(b, s), 0)),
        out_shape=jax.ShapeDtypeStruct((n, GROUP_WIDTH), BF16),
        scratch_shapes=[pltpu.VMEM((D_HEADS, LANES, LANES), F32), pltpu.VMEM((tr, GROUP_WIDTH), F32),
                        pltpu.VMEM((tr, GROUP_WIDTH), F32),
                        pltpu.VMEM((tr // D_CHUNK, range_sums.shape[0], GROUP_WIDTH), F32)],
        compiler_params=_params("parallel", "arbitrary"),
        name="hgrn2",
    )(h, h, h, h, lb_logits, norm_g, jnp.asarray(range_sums, BF16), jnp.asarray(pair_masks))


def _out_proj_kernel(x_ref, a_ref, b_ref, c_ref, d_ref, w_ref, g_ref, bb_ref, o_ref, *, ln_row):
    acc = _dot(a_ref[...], w_ref[0])
    acc += _dot(b_ref[...], w_ref[1])
    acc += _dot(c_ref[...], w_ref[2])
    acc += _dot(d_ref[...], w_ref[3])
    o_ref[...] = _layer_norm(ALPHA * x_ref[...] + acc, _row(g_ref, ln_row), _row(bb_ref, ln_row))


def _out_proj_ln(x, parts, w, g, b, *, layer, ln_row, tm=512):
    n, d = x.shape
    part = pl.BlockSpec((tm, GROUP_WIDTH), lambda i: (i, 0))
    vec = pl.BlockSpec(g.shape, lambda i: (0, 0))
    return pl.pallas_call(
        functools.partial(_out_proj_kernel, ln_row=ln_row),
        grid=(_tiles(n, tm),),
        in_specs=[pl.BlockSpec((tm, d), lambda i: (i, 0)), part, part, part, part,
                  pl.BlockSpec((None, 4, GROUP_WIDTH, d), lambda i: (layer, 0, 0, 0)), vec, vec],
        out_specs=pl.BlockSpec((tm, d), lambda i: (i, 0)),
        out_shape=jax.ShapeDtypeStruct((n, d), F32),
        compiler_params=_params("parallel"),
        name="out_proj_ln",
    )(x, *parts, w, g, b)


def _ple_kernel(x_ref, p_ref, wg_ref, we_ref, g_ref, b_ref, o_ref, *, ln_row):
    x = x_ref[...]
    gate = _sigmoid(_dot(x.astype(BF16), wg_ref[...]))
    emb = _dot(p_ref[...].astype(BF16), we_ref[...])
    o_ref[...] = _layer_norm(ALPHA * x + gate * emb, _row(g_ref, ln_row), _row(b_ref, ln_row))


def _ple_ln(x, p_all, wg, we, g, b, *, layer, ln_row, tm=512):
    n, d = x.shape
    nt = _tiles(n, tm)
    vec = pl.BlockSpec(g.shape, lambda i: (0, 0))
    return pl.pallas_call(
        functools.partial(_ple_kernel, ln_row=ln_row),
        grid=(nt,),
        in_specs=[pl.BlockSpec((tm, d), lambda i: (i, 0)),
                  pl.BlockSpec((tm, PLE_DIM), lambda i: (layer * nt + i, 0)),
                  pl.BlockSpec((None, d, d), lambda i: (layer, 0, 0)),
                  pl.BlockSpec((None, PLE_DIM, d), lambda i: (layer, 0, 0)), vec, vec],
        out_specs=pl.BlockSpec((tm, d), lambda i: (i, 0)),
        out_shape=jax.ShapeDtypeStruct((n, d), F32),
        compiler_params=_params("parallel"),
        name="ple_ln",
    )(x, p_all, wg, we, g, b)


def _mix_in_weight(w):
    nl, d, _ = w.shape

    def even_odd(blk):
        blk = blk.reshape(nl, d, B_HEADS, B_HEAD_DIM // 2, 2)
        return jnp.swapaxes(blk, 3, 4).reshape(nl, d, GROUP_WIDTH)

    q0, k0, v0 = 3 * GROUP_WIDTH, 4 * GROUP_WIDTH, 5 * GROUP_WIDTH
    parts = [w[:, :, :q0], even_odd(w[:, :, q0:k0]), even_odd(w[:, :, k0:v0]), w[:, :, v0:]]
    return jnp.concatenate(parts, axis=2).astype(BF16)


def kernel(x, p, positions, ffn1_w_in, ffn1_w_out, w_mix_in, w_mix_out, rel_bias, diff_lambda, diff_norm_g,
           gmlp_ln_g, gmlp_ln_b, gmlp_w_s, gmlp_b_s, hgrn_lb_logits, hgrn_norm_g, ffn2_w_in, ffn2_w_out,
           ple_w_gate, ple_w_proj, ln_g, ln_b):
    bsz, seq, d = x.shape
    n = bsz * seq
    t_attn = min(512, seq)
    xs = x.reshape(n, d)

    pos_b = jnp.broadcast_to(positions.reshape(n, 1), (n, LANES))
    inv = ROPE_BASE ** (-jnp.linspace(0.0, 1.0, B_HEAD_DIM // 2, dtype=F32))
    inv_b = jnp.concatenate([inv, inv]).reshape(1, LANES)
    cos_t, sin_t = _rope_tables(pos_b, inv_b, tr=min(1024, n))
    bias_tiles = _bias_tiles(rel_bias, t_attn)

    ffn1_in, ffn1_out = ffn1_w_in.astype(BF16), ffn1_w_out.astype(BF16)
    ffn2_in, ffn2_out = ffn2_w_in.astype(BF16), ffn2_w_out.astype(BF16)
    mix_in = _mix_in_weight(w_mix_in)
    mix_out = w_mix_out.astype(BF16).reshape(DEPTH, 4, GROUP_WIDTH, d)
    ple_gate, ple_proj = ple_w_gate.astype(BF16), ple_w_proj.astype(BF16)
    p_all = p.reshape(DEPTH * n, PLE_DIM)
    lng, lnb = ln_g.reshape(DEPTH * 4, d), ln_b.reshape(DEPTH * 4, d)
    lam_all = diff_lambda.reshape(DEPTH * 4, A_HEAD_DIM)
    bs_b = jnp.broadcast_to(gmlp_b_s[:, :, :, None], (DEPTH, C_GROUPS, C_CHUNK, LANES))

    for i in range(DEPTH):
        xs = _ffn_ln(xs, ffn1_in, ffn1_out, lng, lnb, layer=i, ln_row=4 * i)
        h = _in_proj(xs, mix_in, layer=i)
        out_a = _diff_attention(h, bias_tiles, rel_bias, lam_all, diff_norm_g, bsz=bsz, seq=seq, t=t_attn, layer_idx=i)
        out_b = _retention(h, cos_t, sin_t, bsz=bsz, seq=seq)
        out_c = _spatial_gating(h, gmlp_ln_g, gmlp_ln_b, gmlp_w_s, bs_b, layer=i)
        out_d = _hgrn2(h, hgrn_lb_logits, hgrn_norm_g, bsz=bsz, seq=seq, layer_idx=i)
        xs = _out_proj_ln(xs, (out_a, out_b, out_c, out_d), mix_out, lng, lnb, layer=i, ln_row=4 * i + 1)
        xs = _ffn_ln(xs, ffn2_in, ffn2_out, lng, lnb, layer=i, ln_row=4 * i + 2)
        xs = _ple_ln(xs, p_all, ple_gate, ple_proj, lng, lnb, layer=i, ln_row=4 * i + 3)
    return xs.reshape(bsz, seq, d)
```

```python
import functools
import math

import numpy as np
import jax
import jax.numpy as jnp
from jax import lax
from jax.experimental import pallas as pl
from jax.experimental.pallas import tpu as pltpu

D_MODEL = 2048
DEPTH = 2
PLE_DIM = 256
D_FF = 5632
GROUP_WIDTH = 512

A_HEADS = 4
A_HEAD_DIM = 64
N_BUCKETS = 32
MAX_DISTANCE = 128

B_HEADS = 4
B_HEAD_DIM = 128
B_CHUNK = 128
ROPE_BASE = 10000.0

C_GROUPS = 4
C_CHUNK = 128

D_HEADS = 4
D_CHUNK = 64
D_LEVELS = 6

(COL_A_Q, COL_A_K, COL_A_V, COL_B_Q, COL_B_K, COL_B_V, COL_B_G, COL_C_U, COL_C_V,
 COL_D_Q, COL_D_F, COL_D_I, COL_D_G) = range(13)
ALPHA = (2 * DEPTH) ** 0.25
LN_EPS = 1e-5
MASK_VALUE = -1e30
LB_FLOOR = 1e-30
LOG2E = math.log2(math.e)

LANES = 128
V7X_VMEM_BYTES = 64 * 1024 * 1024
VMEM_LIMIT = V7X_VMEM_BYTES - 8 * 1024 * 1024
FFN_VMEM_LIMIT = V7X_VMEM_BYTES - 1024 * 1024

F32 = jnp.float32
BF16 = jnp.bfloat16


def _dot(a, b):
    return jnp.dot(a, b, preferred_element_type=F32)


def _dot_nt(a, b):
    return lax.dot_general(a, b, (((1,), (1,)), ((), ())), preferred_element_type=F32)


def _sigmoid(x):
    return 1.0 / (1.0 + jnp.exp(-x))


def _layer_norm(y, g, b):
    mu = jnp.mean(y, axis=-1, keepdims=True)
    d = y - mu
    var = jnp.mean(d * d, axis=-1, keepdims=True)
    return d * lax.rsqrt(var + LN_EPS) * g + b


def _tiles(total, tile):
    assert total % tile == 0, f"{total} rows/columns do not split into tiles of {tile}"
    return total // tile


def _params(*sem, vmem=VMEM_LIMIT):
    return pltpu.CompilerParams(dimension_semantics=sem, vmem_limit_bytes=vmem)


def _row(ref, r):
    return ref[r:r + 1, :]


def _ffn_kernel(x_ref, wg_ref, wu_ref, wo_ref, g_ref, b_ref, o_ref, xb_ref, *, ln_row):
    j = pl.program_id(1)

    @pl.when(j == 0)
    def _():
        xb_ref[...] = x_ref[...].astype(BF16)
        o_ref[...] = jnp.zeros_like(o_ref)

    xb = xb_ref[...]
    hg = _dot(xb, wg_ref[...])
    hu = _dot(xb, wu_ref[...])
    a = (hg * _sigmoid(hg) * hu).astype(BF16)
    o_ref[...] += _dot(a, wo_ref[...])

    @pl.when(j == pl.num_programs(1) - 1)
    def _():
        y = ALPHA * x_ref[...] + 0.5 * o_ref[...]
        o_ref[...] = _layer_norm(y, _row(g_ref, ln_row), _row(b_ref, ln_row))


def _ffn_ln(x, w_in, w_out, g, b, *, layer, ln_row, tm=1024, tf=512):
    n, d = x.shape
    f = w_out.shape[1]
    nf = _tiles(f, tf)
    ln_spec = pl.BlockSpec(g.shape, lambda i, j: (0, 0))
    return pl.pallas_call(
        functools.partial(_ffn_kernel, ln_row=ln_row),
        grid=(_tiles(n, tm), nf),
        in_specs=[
            pl.BlockSpec((tm, d), lambda i, j: (i, 0)),
            pl.BlockSpec((None, d, tf), lambda i, j: (layer, 0, j)),
            pl.BlockSpec((None, d, tf), lambda i, j: (layer, 0, j + nf)),
            pl.BlockSpec((None, tf, d), lambda i, j: (layer, j, 0)),
            ln_spec, ln_spec,
        ],
        out_specs=pl.BlockSpec((tm, d), lambda i, j: (i, 0)),
        out_shape=jax.ShapeDtypeStruct((n, d), F32),
        scratch_shapes=[pltpu.VMEM((tm, d), BF16)],
        compiler_params=_params("parallel", "arbitrary", vmem=FFN_VMEM_LIMIT),
        name="ffn_ln",
    )(x, w_in, w_in, w_out, g, b)


def _in_proj_kernel(x_ref, w_ref, o_ref, xb_ref):
    @pl.when(pl.program_id(1) == 0)
    def _():
        xb_ref[...] = x_ref[...].astype(BF16)

    o_ref[...] = _dot(xb_ref[...], w_ref[...])


def _in_proj(x, w, *, layer, tm=2048, tn=512):
    n, d = x.shape
    c = w.shape[2]
    return pl.pallas_call(
        _in_proj_kernel,
        grid=(_tiles(n, tm), _tiles(c, tn)),
        in_specs=[
            pl.BlockSpec((tm, d), lambda i, j: (i, 0)),
            pl.BlockSpec((None, d, tn), lambda i, j: (layer, 0, j)),
        ],
        out_specs=pl.BlockSpec((tm, tn), lambda i, j: (i, j)),
        out_shape=jax.ShapeDtypeStruct((n, c), F32),
        scratch_shapes=[pltpu.VMEM((tm, d), BF16)],
        compiler_params=_params("parallel", "arbitrary"),
        name="in_proj",
    )(x, w)


def _bias_tiles_kernel(rb_ref, o_ref, *, t):
    h = pl.program_id(0)
    j = lax.broadcasted_iota(jnp.int32, (t, t), 0)
    i = lax.broadcasted_iota(jnp.int32, (t, t), 1)
    max_exact = N_BUCKETS // 2
    for sel in range(2):
        n = jnp.maximum(i - j + sel * t, 0)
        nf = jnp.maximum(n, 1).astype(F32)
        large = max_exact + (jnp.log(nf / max_exact) / math.log(MAX_DISTANCE / max_exact)
                             * (N_BUCKETS - max_exact)).astype(jnp.int32)
        large = jnp.minimum(large, N_BUCKETS - 1)
        bucket = jnp.where(n < max_exact, n, large)
        val = jnp.zeros((t, t), F32)
        for bkt in range(N_BUCKETS):
            val = jnp.where(bucket == bkt, rb_ref[bkt, h] * LOG2E, val)
        if sel == 0:
            val = jnp.where(i >= j, val, MASK_VALUE)
        o_ref[0, sel] = val


def _bias_tiles(rel_bias, t):
    return pl.pallas_call(
        functools.partial(_bias_tiles_kernel, t=t),
        grid=(A_HEADS,),
        in_specs=[pl.BlockSpec(memory_space=pltpu.SMEM)],
        out_specs=pl.BlockSpec((1, 2, t, t), lambda h: (h, 0, 0, 0)),
        out_shape=jax.ShapeDtypeStruct((A_HEADS, 2, t, t), F32),
        compiler_params=_params("arbitrary"),
        name="t5_bias_tiles",
    )(rel_bias)


def _attn_kernel(q_ref, k_ref, v_ref, bt_ref, rb_ref, lam_ref, ng_ref, o_ref,
                 kb_ref, vt_ref, qst_ref, sa_ref, sb_ref, m_ref, l_ref, acc_ref, *, t, layer, lam_init):
    h = pl.program_id(1)
    nq = kb_ref.shape[0]

    def fill(c, carry):
        rows = pl.ds(pl.multiple_of(c * t, t), t)
        kb_ref[c] = k_ref[rows, :].astype(BF16)
        vt_ref[c] = v_ref[rows, :].T.astype(BF16)
        return carry

    lax.fori_loop(0, nq, fill, 0)

    def scores(kj, s2_ref):
        s2_ref[...] = _dot(kb_ref[kj], qst_ref[...])

    def open_query_block(i):
        rows = pl.ds(pl.multiple_of(i * t, t), t)
        qt = (q_ref[rows, :] * (A_HEAD_DIM ** -0.5 * LOG2E)).T
        feat = lax.broadcasted_iota(jnp.int32, qt.shape, 0)
        qst_ref[:, 0:t] = jnp.where(feat < A_HEAD_DIM, qt, 0.0).astype(BF16)
        qst_ref[:, t:2 * t] = jnp.where(feat >= A_HEAD_DIM, qt, 0.0).astype(BF16)
        scores(0, sa_ref)

    def step(kj, s2_ref, tile_sel, const_bias):
        vt = vt_ref[kj]
        for c in range(2):
            s = s2_ref[:, c * t:(c + 1) * t]
            if tile_sel is not None:
                s = s + bt_ref[0, tile_sel]
            m_old = m_ref[c]
            smax = jnp.max(s, axis=0, keepdims=True)
            if const_bias is not None:
                smax = smax + const_bias
            m_new = jnp.maximum(m_old, smax)
            shift = m_new if const_bias is None else m_new - const_bias
            p = jnp.exp2(s - shift)
            corr = jnp.exp2(m_old - m_new)
            l_ref[c] = corr * l_ref[c] + jnp.sum(p, axis=0, keepdims=True)
            acc_ref[c] = corr * acc_ref[c] + _dot(vt, p.astype(BF16))
            m_ref[c] = m_new

    far_bias = rb_ref[N_BUCKETS - 1, h] * LOG2E

    lq1, lk1, lq2, lk2 = (_row(lam_ref, 4 * layer + r) for r in range(4))
    lam = (jnp.exp(jnp.sum(lq1 * lk1, axis=1, keepdims=True))
           - jnp.exp(jnp.sum(lq2 * lk2, axis=1, keepdims=True)) + lam_init)

    def query_block(i, carry):
        m_ref[...] = jnp.full(m_ref.shape, MASK_VALUE, F32)
        l_ref[...] = jnp.zeros_like(l_ref)
        acc_ref[...] = jnp.zeros_like(acc_ref)

        n_far = i - 1
        odd_far = jnp.logical_and(i >= 2, n_far % 2 == 1)

        def far_pair(pj, inner):
            scores(2 * pj + 1, sb_ref)
            step(2 * pj, sa_ref, None, far_bias)
            scores(2 * pj + 2, sa_ref)
            step(2 * pj + 1, sb_ref, None, far_bias)
            return inner

        lax.fori_loop(0, jnp.maximum(n_far, 0) // 2, far_pair, 0)

        @pl.when(odd_far)
        def _():
            scores(i - 1, sb_ref)
            step(i - 2, sa_ref, None, far_bias)
            scores(i, sa_ref)
            step(i - 1, sb_ref, 1, None)
            step(i, sa_ref, 0, None)

        @pl.when(jnp.logical_and(i >= 1, jnp.logical_not(odd_far)))
        def _():
            scores(i, sb_ref)
            step(i - 1, sa_ref, 1, None)
            step(i, sb_ref, 0, None)

        @pl.when(i == 0)
        def _():
            step(0, sa_ref, 0, None)

        open_query_block(jnp.minimum(i + 1, nq - 1))
        o = (acc_ref[0] / l_ref[0] - lam * (acc_ref[1] / l_ref[1])).T
        o = o * lax.rsqrt(jnp.mean(o * o, axis=-1, keepdims=True) + LN_EPS) * _row(ng_ref, layer)
        o_ref[pl.ds(pl.multiple_of(i * t, t), t), :] = (o * (1.0 - lam_init)).astype(o_ref.dtype)
        return carry

    open_query_block(0)
    lax.fori_loop(0, nq, query_block, 0)


def _diff_attention(h, bias_tiles, rel_bias, diff_lambda, norm_g, *, bsz, seq, t, layer_idx):
    n = bsz * seq
    nq = _tiles(seq, t)
    lam_init = 0.8 - 0.6 * math.exp(-0.3 * layer_idx)
    per_block = GROUP_WIDTH // LANES
    return pl.pallas_call(
        functools.partial(_attn_kernel, t=t, layer=layer_idx, lam_init=lam_init),
        grid=(bsz, A_HEADS),
        in_specs=[
            pl.BlockSpec((seq, LANES), lambda b, hh: (b, COL_A_Q * per_block + hh)),
            pl.BlockSpec((seq, LANES), lambda b, hh: (b, COL_A_K * per_block + hh)),
            pl.BlockSpec((seq, LANES), lambda b, hh: (b, COL_A_V * per_block + hh)),
            pl.BlockSpec((1, 2, t, t), lambda b, hh: (hh, 0, 0, 0)),
            pl.BlockSpec(memory_space=pltpu.SMEM),
            pl.BlockSpec(diff_lambda.shape, lambda b, hh: (0, 0)),
            pl.BlockSpec(norm_g.shape, lambda b, hh: (0, 0)),
        ],
        out_specs=pl.BlockSpec((seq, LANES), lambda b, hh: (b, hh)),
        out_shape=jax.ShapeDtypeStruct((n, GROUP_WIDTH), BF16),
        scratch_shapes=[
            pltpu.VMEM((nq, t, LANES), BF16), pltpu.VMEM((nq, LANES, t), BF16),
            pltpu.VMEM((LANES, 2 * t), BF16), pltpu.VMEM((t, 2 * t), F32), pltpu.VMEM((t, 2 * t), F32),
            pltpu.VMEM((2, 1, t), F32), pltpu.VMEM((2, 1, t), F32), pltpu.VMEM((2, LANES, t), F32),
        ],
        compiler_params=_params("parallel", "parallel"),
        name="diff_attention",
    )(h, h, h, bias_tiles, rel_bias, diff_lambda, norm_g)


def _rope_kernel(pos_ref, inv_ref, cos_ref, sin_ref):
    ang = pos_ref[...].astype(F32) * inv_ref[...]
    lane = lax.broadcasted_iota(jnp.int32, ang.shape, 1)
    cos_ref[...] = jnp.cos(ang)
    sn = jnp.sin(ang)
    sin_ref[...] = jnp.where(lane < B_HEAD_DIM // 2, -sn, sn)


def _rope_tables(pos_b, inv_b, *, tr=1024):
    n = pos_b.shape[0]
    return pl.pallas_call(
        _rope_kernel,
        grid=(_tiles(n, tr),),
        in_specs=[pl.BlockSpec((tr, LANES), lambda i: (i, 0)), pl.BlockSpec((1, LANES), lambda i: (0, 0))],
        out_specs=[pl.BlockSpec((tr, LANES), lambda i: (i, 0))] * 2,
        out_shape=[jax.ShapeDtypeStruct((n, LANES), F32)] * 2,
        compiler_params=_params("parallel"),
        name="rope_tables",
    )(pos_b, inv_b)


def _retention_consts():
    log_g = np.log(1.0 - 2.0 ** (-5.0 - np.arange(B_HEADS, dtype=np.float64)))
    j = np.arange(B_CHUNK, dtype=np.float64)
    diff = j[:, None] - j[None, :]
    decay_mask = np.where(diff >= 0, np.exp(log_g[:, None, None] * np.maximum(diff, 0.0)), 0.0)
    q_dec = np.exp(log_g[:, None] * (j[None, :] + 1.0))
    k_dec = np.exp(log_g[:, None] * (B_CHUNK - 1.0 - j[None, :]))
    chunk_dec = np.exp(log_g * B_CHUNK)
    bc = lambda a: np.ascontiguousarray(np.broadcast_to(a[:, :, None], (B_HEADS, B_CHUNK, LANES)))
    return (decay_mask.astype(np.float32), bc(q_dec).astype(np.float32), bc(k_dec).astype(np.float32),
            [float(c) for c in chunk_dec])


def _retention_kernel(q_ref, k_ref, v_ref, g_ref, cos_ref, sin_ref, dm_ref, qd_ref, kd_ref, o_ref, st_ref,
                      *, chunk_dec):
    @pl.when(pl.program_id(1) == 0)
    def _():
        st_ref[...] = jnp.zeros_like(st_ref)

    tr = q_ref.shape[0]
    half = B_HEAD_DIM // 2
    for c in range(tr // B_CHUNK):
        rows = slice(c * B_CHUNK, (c + 1) * B_CHUNK)
        cos = cos_ref[rows, :]
        sin = sin_ref[rows, :]
        for hh in range(B_HEADS):
            cols = slice(hh * B_HEAD_DIM, (hh + 1) * B_HEAD_DIM)
            q = q_ref[rows, cols]
            k = k_ref[rows, cols]
            q = q * cos + pltpu.roll(q, half, 1) * sin
            k = (k * cos + pltpu.roll(k, half, 1) * sin) * (B_HEAD_DIM ** -0.5)
            qb = q.astype(BF16)
            vb = v_ref[rows, cols].astype(BF16)
            scores = _dot_nt(qb, k.astype(BF16)) * dm_ref[hh]
            st = st_ref[hh]
            o = _dot(scores.astype(BF16), vb) + _dot(qb, st.astype(BF16)) * qd_ref[hh]
            st_ref[hh] = st * chunk_dec[hh] + _dot((k * kd_ref[hh]).T.astype(BF16), vb)
            mu = jnp.mean(o, axis=-1, keepdims=True)
            d = o - mu
            var = jnp.mean(d * d, axis=-1, keepdims=True)
            gate = g_ref[rows, cols]
            o_ref[rows, cols] = (d * lax.rsqrt(var + LN_EPS) * (gate * _sigmoid(gate))).astype(o_ref.dtype)


def _retention(h, cos_t, sin_t, *, bsz, seq, tr=512):
    n = bsz * seq
    ns = _tiles(seq, tr)
    dm, qd, kd, chunk_dec = _retention_consts()
    row = lambda b, s: b * ns + s
    const3 = pl.BlockSpec((B_HEADS, B_CHUNK, LANES), lambda b, s: (0, 0, 0))
    return pl.pallas_call(
        functools.partial(_retention_kernel, chunk_dec=chunk_dec),
        grid=(bsz, ns),
        in_specs=[
            pl.BlockSpec((tr, GROUP_WIDTH), lambda b, s: (row(b, s), COL_B_Q)),
            pl.BlockSpec((tr, GROUP_WIDTH), lambda b, s: (row(b, s), COL_B_K)),
            pl.BlockSpec((tr, GROUP_WIDTH), lambda b, s: (row(b, s), COL_B_V)),
            pl.BlockSpec((tr, GROUP_WIDTH), lambda b, s: (row(b, s), COL_B_G)),
            pl.BlockSpec((tr, LANES), lambda b, s: (row(b, s), 0)),
            pl.BlockSpec((tr, LANES), lambda b, s: (row(b, s), 0)),
            const3, const3, const3,
        ],
        out_specs=pl.BlockSpec((tr, GROUP_WIDTH), lambda b, s: (row(b, s), 0)),
        out_shape=jax.ShapeDtypeStruct((n, GROUP_WIDTH), BF16),
        scratch_shapes=[pltpu.VMEM((B_HEADS, B_HEAD_DIM, B_HEAD_DIM), F32)],
        compiler_params=_params("parallel", "arbitrary"),
        name="retention",
    )(h, h, h, h, cos_t, sin_t, jnp.asarray(dm), jnp.asarray(qd), jnp.asarray(kd))


def _gelu(x):
    return 0.5 * x * (1.0 + lax.erf(x * (0.5 ** 0.5)))


def _gmlp_kernel(u_ref, v_ref, g_ref, b_ref, ws_ref, bs_ref, o_ref, *, layer):
    tr = u_ref.shape[0]
    v = _layer_norm(_gelu(v_ref[...]), _row(g_ref, layer), _row(b_ref, layer)).astype(BF16)
    ti = lax.broadcasted_iota(jnp.int32, (C_CHUNK, C_CHUNK), 0)
    si = lax.broadcasted_iota(jnp.int32, (C_CHUNK, C_CHUNK), 1)
    for gi in range(C_GROUPS):
        cols = slice(gi * LANES, (gi + 1) * LANES)
        w = jnp.where(ti >= si, ws_ref[gi], 0.0).astype(BF16)
        for c in range(tr // C_CHUNK):
            rows = slice(c * C_CHUNK, (c + 1) * C_CHUNK)
            mixed = _dot(w, v[rows, cols]) + bs_ref[gi]
            o_ref[rows, cols] = (_gelu(u_ref[rows, cols]) * mixed).astype(o_ref.dtype)


def _spatial_gating(h, ln_g, ln_b, w_s, bs_b, *, layer, tr=512):
    n = h.shape[0]
    const3 = pl.BlockSpec((None, C_GROUPS, C_CHUNK, LANES), lambda i: (layer, 0, 0, 0))
    vec = pl.BlockSpec(ln_g.shape, lambda i: (0, 0))
    return pl.pallas_call(
        functools.partial(_gmlp_kernel, layer=layer),
        grid=(_tiles(n, tr),),
        in_specs=[
            pl.BlockSpec((tr, GROUP_WIDTH), lambda i: (i, COL_C_U)),
            pl.BlockSpec((tr, GROUP_WIDTH), lambda i: (i, COL_C_V)),
            vec, vec, const3, const3,
        ],
        out_specs=pl.BlockSpec((tr, GROUP_WIDTH), lambda i: (i, 0)),
        out_shape=jax.ShapeDtypeStruct((n, GROUP_WIDTH), BF16),
        compiler_params=_params("parallel"),
        name="spatial_gating",
    )(h, h, ln_g, ln_b, w_s, bs_b)


def _hgrn_consts():
    c = D_CHUNK
    t = np.arange(c)[:, None]
    r = np.arange(c)[None, :]
    ranges = [r <= t, r > t]
    masks = [r == t]
    m = c // 2
    while m >= 1:
        ref = (t // (2 * m)) * 2 * m + m - 1
        second = (t % (2 * m)) >= m
        ranges.append(np.where(second, (r > ref) & (r <= t), (r > t) & (r <= ref)))
        masks.append(((t // (2 * m)) == (r // (2 * m))) & second & ((r % (2 * m)) < m))
        m //= 2
    ranges = np.concatenate(ranges, axis=0)
    return (np.concatenate([ranges, ranges], axis=1).astype(np.float32), np.stack(masks).astype(np.float32))


def _hgrn_kernel(q_ref, z_ref, i_ref, g_ref, lbl_ref, ng_ref, rs_ref, pm_ref, o_ref,
                 st_ref, oc_ref, key_ref, ex_ref, *, layer_idx):
    @pl.when(pl.program_id(1) == 0)
    def _():
        st_ref[...] = jnp.zeros_like(st_ref)

    tr = q_ref.shape[0]
    cs = D_CHUNK

    logits = lbl_ref[...]
    e = jnp.exp(logits - jnp.max(logits, axis=0, keepdims=True))
    soft = e / jnp.sum(e, axis=0, keepdims=True)
    lb = jnp.sum(soft[0:layer_idx + 1], axis=0, keepdims=True) - soft[0:1]
    lb = jnp.maximum(lb, LB_FLOOR)

    for c in range(tr // cs):
        rows = slice(c * cs, (c + 1) * cs)
        z = z_ref[rows, :]
        ez = jnp.exp(-jnp.abs(z))
        r = 1.0 / (1.0 + ez)
        sig_pos = jnp.where(z >= 0, r, ez * r)
        sig_neg = jnp.where(z >= 0, ez * r, r)
        log2_f = jnp.log(sig_pos + lb * sig_neg) * LOG2E
        key_ref[rows, :] = (1.0 - lb) * sig_neg
        hi = log2_f.astype(BF16)
        lo = (log2_f - hi.astype(F32)).astype(BF16)
        ex_ref[c] = _dot(rs_ref[...], jnp.concatenate([hi, lo], axis=0))

    for c in range(tr // cs):
        rows = slice(c * cs, (c + 1) * cs)
        for hh in range(D_HEADS):
            cols = slice(hh * LANES, (hh + 1) * LANES)
            q = q_ref[rows, cols]
            k = key_ref[rows, cols]
            vb = i_ref[rows, cols].astype(BF16)
            e_all = ex_ref.at[c]
            b = e_all[0:cs, cols]
            rem = e_all[cs:2 * cs, cols]
            st = st_ref[hh]
            o = _dot_nt((q * jnp.exp2(b)).astype(BF16), st.astype(BF16))
            a = jnp.where(pm_ref[0] > 0, _dot_nt(q.astype(BF16), k.astype(BF16)), 0.0)
            for lv in range(D_LEVELS):
                w = jnp.exp2(e_all[(2 + lv) * cs:(3 + lv) * cs, cols])
                a = a + jnp.where(pm_ref[1 + lv] > 0, _dot_nt((q * w).astype(BF16), (k * w).astype(BF16)), 0.0)
            oc_ref[rows, cols] = o + _dot(a.astype(BF16), vb)
            kdec = (k * jnp.exp2(rem)).astype(BF16)
            st_ref[hh] = st * jnp.exp2(b[cs - 1:cs, :]) + _dot(i_ref[rows, cols].T.astype(BF16), kdec)

    o = oc_ref[...]
    o = o * lax.rsqrt(jnp.mean(o * o, axis=-1, keepdims=True) + LN_EPS) * _row(ng_ref, layer_idx)
    gate = g_ref[...]
    o_ref[...] = (o * (gate * _sigmoid(gate))).astype(o_ref.dtype)


def _hgrn2(h, lb_logits, norm_g, *, bsz, seq, layer_idx, tr=512):
    n = bsz * seq
    ns = _tiles(seq, tr)
    row = lambda b, s: b * ns + s
    range_sums, pair_masks = _hgrn_consts()
    return pl.pallas_call(
        functools.partial(_hgrn_kernel, layer_idx=layer_idx),
        grid=(bsz, ns),
        in_specs=[
            pl.BlockSpec((tr, GROUP_WIDTH), lambda b, s: (row(b, s), COL_D_Q)),
            pl.BlockSpec((tr, GROUP_WIDTH), lambda b, s: (row(b, s), COL_D_F)),
            pl.BlockSpec((tr, GROUP_WIDTH), lambda b, s: (row(b, s), COL_D_I)),
            pl.BlockSpec((tr, GROUP_WIDTH), lambda b, s: (row(b, s), COL_D_G)),
            pl.BlockSpec((DEPTH, GROUP_WIDTH), lambda b, s: (0, 0)),
            pl.BlockSpec((DEPTH, GROUP_WIDTH), lambda b, s: (0, 0)),
            pl.BlockSpec(range_sums.shape, lambda b, s: (0, 0)),
            pl.BlockSpec(pair_masks.shape, lambda b, s: (0, 0, 0)),
        ],
        out_specs=pl.BlockSpec((tr, GROUP_WIDTH), lambda b, s: (row(b, s), 0)),
        out_shape=jax.ShapeDtypeStruct((n, GROUP_WIDTH), BF16),
        scratch_shapes=[pltpu.VMEM((D_HEADS, LANES, LANES), F32), pltpu.VMEM((tr, GROUP_WIDTH), F32),
                        pltpu.VMEM((tr, GROUP_WIDTH), F32),
                        pltpu.VMEM((tr // D_CHUNK, range_sums.shape[0], GROUP_WIDTH), F32)],
        compiler_params=_params("parallel", "arbitrary"),
        name="hgrn2",
    )(h, h, h, h, lb_logits, norm_g, jnp.asarray(range_sums, BF16), jnp.asarray(pair_masks))


def _out_proj_kernel(x_ref, a_ref, b_ref, c_ref, d_ref, w_ref, g_ref, bb_ref, o_ref, *, ln_row):
    acc = _dot(a_ref[...], w_ref[0])
    acc += _dot(b_ref[...], w_ref[1])
    acc += _dot(c_ref[...], w_ref[2])
    acc += _dot(d_ref[...], w_ref[3])
    o_ref[...] = _layer_norm(ALPHA * x_ref[...] + acc, _row(g_ref, ln_row), _row(bb_ref, ln_row))


def _out_proj_ln(x, parts, w, g, b, *, layer, ln_row, tm=512):
    n, d = x.shape
    part = pl.BlockSpec((tm, GROUP_WIDTH), lambda i: (i, 0))
    vec = pl.BlockSpec(g.shape, lambda i: (0, 0))
    return pl.pallas_call(
        functools.partial(_out_proj_kernel, ln_row=ln_row),
        grid=(_tiles(n, tm),),
        in_specs=[pl.BlockSpec((tm, d), lambda i: (i, 0)), part, part, part, part,
                  pl.BlockSpec((None, 4, GROUP_WIDTH, d), lambda i: (layer, 0, 0, 0)), vec, vec],
        out_specs=pl.BlockSpec((tm, d), lambda i: (i, 0)),
        out_shape=jax.ShapeDtypeStruct((n, d), F32),
        compiler_params=_params("parallel"),
        name="out_proj_ln",
    )(x, *parts, w, g, b)


def _ple_kernel(x_ref, p_ref, wg_ref, we_ref, g_ref, b_ref, o_ref, *, ln_row):
    x = x_ref[...]
    gate = _sigmoid(_dot(x.astype(BF16), wg_ref[...]))
    emb = _dot(p_ref[...].astype(BF16), we_ref[...])
    o_ref[...] = _layer_norm(ALPHA * x + gate * emb, _row(g_ref, ln_row), _row(b_ref, ln_row))


def _ple_ln(x, p_all, wg, we, g, b, *, layer, ln_row, tm=512):
    n, d = x.shape
    nt = _tiles(n, tm)
    vec = pl.BlockSpec(g.shape, lambda i: (0, 0))
    return pl.pallas_call(
        functools.partial(_ple_kernel, ln_row=ln_row),
        grid=(nt,),
        in_specs=[pl.BlockSpec((tm, d), lambda i: (i, 0)),
                  pl.BlockSpec((tm, PLE_DIM), lambda i: (layer * nt + i, 0)),
                  pl.BlockSpec((None, d, d), lambda i: (layer, 0, 0)),
                  pl.BlockSpec((None, PLE_DIM, d), lambda i: (layer, 0, 0)), vec, vec],
        out_specs=pl.BlockSpec((tm, d), lambda i: (i, 0)),
        out_shape=jax.ShapeDtypeStruct((n, d), F32),
        compiler_params=_params("parallel"),
        name="ple_ln",
    )(x, p_all, wg, we, g, b)


def _mix_in_weight(w):
    nl, d, _ = w.shape

    def even_odd(blk):
        blk = blk.reshape(nl, d, B_HEADS, B_HEAD_DIM // 2, 2)
        return jnp.swapaxes(blk, 3, 4).reshape(nl, d, GROUP_WIDTH)

    q0, k0, v0 = COL_B_Q * GROUP_WIDTH, COL_B_K * GROUP_WIDTH, COL_B_V * GROUP_WIDTH
    parts = [w[:, :, :q0], even_odd(w[:, :, q0:k0]), even_odd(w[:, :, k0:v0]), w[:, :, v0:]]
    return jnp.concatenate(parts, axis=2).astype(BF16)


def kernel(x, p, positions, ffn1_w_in, ffn1_w_out, w_mix_in, w_mix_out, rel_bias, diff_lambda, diff_norm_g,
           gmlp_ln_g, gmlp_ln_b, gmlp_w_s, gmlp_b_s, hgrn_lb_logits, hgrn_norm_g, ffn2_w_in, ffn2_w_out,
           ple_w_gate, ple_w_proj, ln_g, ln_b):
    bsz, seq, d = x.shape
    n = bsz * seq
    t_attn = min(512, seq)
    xs = x.reshape(n, d)

    pos_b = jnp.broadcast_to(positions.reshape(n, 1), (n, LANES))
    inv = ROPE_BASE ** (-jnp.linspace(0.0, 1.0, B_HEAD_DIM // 2, dtype=F32))
    inv_b = jnp.concatenate([inv, inv]).reshape(1, LANES)
    cos_t, sin_t = _rope_tables(pos_b, inv_b, tr=min(1024, n))
    bias_tiles = _bias_tiles(rel_bias, t_attn)

    ffn1_in, ffn1_out = ffn1_w_in.astype(BF16), ffn1_w_out.astype(BF16)
    ffn2_in, ffn2_out = ffn2_w_in.astype(BF16), ffn2_w_out.astype(BF16)
    mix_in = _mix_in_weight(w_mix_in)
    mix_out = w_mix_out.astype(BF16).reshape(DEPTH, 4, GROUP_WIDTH, d)
    ple_gate, ple_proj = ple_w_gate.astype(BF16), ple_w_proj.astype(BF16)
    p_all = p.reshape(DEPTH * n, PLE_DIM)
    lng, lnb = ln_g.reshape(DEPTH * 4, d), ln_b.reshape(DEPTH * 4, d)
    lam_all = diff_lambda.reshape(DEPTH * 4, A_HEAD_DIM)
    bs_b = jnp.broadcast_to(gmlp_b_s[:, :, :, None], (DEPTH, C_GROUPS, C_CHUNK, LANES))

    for i in range(DEPTH):
        xs = _ffn_ln(xs, ffn1_in, ffn1_out, lng, lnb, layer=i, ln_row=4 * i)
        h = _in_proj(xs, mix_in, layer=i)
        out_a = _diff_attention(h, bias_tiles, rel_bias, lam_all, diff_norm_g, bsz=bsz, seq=seq, t=t_attn, layer_idx=i)
        out_b = _retention(h, cos_t, sin_t, bsz=bsz, seq=seq)
        out_c = _spatial_gating(h, gmlp_ln_g, gmlp_ln_b, gmlp_w_s, bs_b, layer=i)
        out_d = _hgrn2(h, hgrn_lb_logits, hgrn_norm_g, bsz=bsz, seq=seq, layer_idx=i)
        xs = _out_proj_ln(xs, (out_a, out_b, out_c, out_d), mix_out, lng, lnb, layer=i, ln_row=4 * i + 1)
        xs = _ffn_ln(xs, ffn2_in, ffn2_out, lng, lnb, layer=i, ln_row=4 * i + 2)
        xs = _ple_ln(xs, p_all, ple_gate, ple_proj, lng, lnb, layer=i, ln_row=4 * i + 3)
    return xs.reshape(bsz, seq, d)
```
